```python
import math
import jax, jax.numpy as jnp
from jax import lax
import numpy as np

D_MODEL = 4096
BATCH = 1
SEQ = 8192
DEPTH = 4

HEAD_DIM = 128
N_A_LAYERS = max(1, DEPTH // 2)
N_B_LAYERS = DEPTH - N_A_LAYERS
A_GROUPS = ((128, 1), (512, 4), (2048, 16))
A_HEADS = D_MODEL // 256
A_WIDTH = A_HEADS * HEAD_DIM
A_BLOCK = 128
B_HEADS = D_MODEL // HEAD_DIM
B_KV_HEADS = B_HEADS // 4
B_REP = B_HEADS // B_KV_HEADS
MOBA_BLOCK = 256
MOBA_TOPK = 3
MOBA_Q_CHUNK = 16
MOE_GROUPS = 4
MOE_EXPERTS_PER_GROUP = 8
N_EXPERTS = MOE_GROUPS * MOE_EXPERTS_PER_GROUP
D_EXPERT = D_MODEL // 16
MOE_TOPK = 2
MOE_BLOCK = 128
ADA_RANK = D_MODEL // 16
ALPHA = (2 * DEPTH) ** 0.25
BETA = (8 * DEPTH) ** -0.25
LN_EPS = 1e-5
NEG = -1e30

kernel_name = "yoco_dilated_moba_hmoe_trunk"


def layer_norm(x, gain, bias):
    xf = x.astype(jnp.float32)
    mu = jnp.mean(xf, axis=-1, keepdims=True)
    var = jnp.mean(jnp.square(xf - mu), axis=-1, keepdims=True)
    return ((xf - mu) * lax.rsqrt(var + LN_EPS)).astype(x.dtype) * gain + bias


def alibi_slopes(n):
    return jnp.exp2(-8.0 * jnp.arange(1, n + 1, dtype=jnp.float32) / n)


def modulation(c, w_down, w_up, b_up, n_parts):
    m = (jax.nn.silu(c) @ w_down) @ w_up + b_up
    return [t[:, None, :] for t in jnp.split(m, n_parts, axis=-1)]


def dilated_branch(q, k, v, n_steps, dilation, slopes):
    b, s, h, e = q.shape
    n = s // dilation
    nb = -(-n // A_BLOCK)
    n_pad = nb * A_BLOCK

    def strided(t):
        t = t.reshape(b, n, dilation, h, e).transpose(0, 2, 1, 3, 4)
        t = jnp.pad(t, ((0, 0), (0, 0), (0, n_pad - n), (0, 0), (0, 0)))
        return t.reshape(b, dilation, nb, A_BLOCK, h, e)

    def band(t):
        prev = jnp.pad(t, ((0, 0), (0, 0), (1, 0), (0, 0), (0, 0), (0, 0)))[:, :, :nb]
        return jnp.concatenate([prev, t], axis=3)

    qs = strided(q)
    kb = band(strided(k))
    vb = band(strided(v))
    sc = jnp.einsum('brnqhe,brnkhe->brnhqk', qs, kb,
                    preferred_element_type=jnp.float32) / math.sqrt(e)
    kj = jnp.arange(2 * A_BLOCK)
    step = jnp.arange(A_BLOCK)[:, None] + A_BLOCK - kj[None, :]
    in_band = (step >= 0) & (step <= n_steps)
    exists = (jnp.arange(nb)[:, None, None] > 0) | (kj[None, None, :] >= A_BLOCK)
    valid = in_band[None] & exists
    bias = -slopes[:, None, None] * (step * dilation).astype(jnp.float32)[None]
    sc = jnp.where(valid[:, None], sc + bias, NEG)
    lse = jax.nn.logsumexp(sc, axis=-1)
    p = jnp.exp(sc - lse[..., None]).astype(v.dtype)
    o = jnp.einsum('brnhqk,brnkhe->brnqhe', p, vb)
    o = o.reshape(b, dilation, n_pad, h, e)[:, :, :n].transpose(0, 2, 1, 3, 4).reshape(b, s, h, e)
    lse = lse.transpose(0, 1, 2, 4, 3).reshape(b, dilation, n_pad, h)[:, :, :n]
    lse = lse.transpose(0, 2, 1, 3).reshape(b, s, h)
    return o, lse


def dilated_mixture_attention(h, w_qkv, w_o):
    b, s, _ = h.shape
    qkv = (h @ w_qkv).reshape(b, s, len(A_GROUPS), 3, A_HEADS, HEAD_DIM)
    slopes = alibi_slopes(A_HEADS)
    outs, lses = [], []
    for g, (window, dilation) in enumerate(A_GROUPS):
        o, lse = dilated_branch(qkv[:, :, g, 0], qkv[:, :, g, 1], qkv[:, :, g, 2],
                                window // dilation, dilation, slopes)
        outs.append(o)
        lses.append(lse)
    wts = jax.nn.softmax(jnp.stack(lses), axis=0)
    o = jnp.sum(jnp.stack(outs).astype(jnp.float32) * wts[..., None], axis=0).astype(h.dtype)
    return o.reshape(b, s, A_WIDTH) @ w_o


def shared_kv(x, c, kv_ada_down, kv_ada_up, kv_ada_bias, kv_w):
    shift, scale = modulation(c, kv_ada_down, kv_ada_up, kv_ada_bias, 2)
    hk = x * (1 + scale) + shift
    b, s, _ = x.shape
    kv = (hk @ kv_w).reshape(b, s, 2, B_KV_HEADS, HEAD_DIM)
    n_blk = -(-s // MOBA_BLOCK)
    pad = ((0, 0), (0, n_blk * MOBA_BLOCK - s), (0, 0), (0, 0))
    k = jnp.pad(kv[:, :, 0], pad)
    v = jnp.pad(kv[:, :, 1], pad)
    k_mean = k.astype(jnp.float32).reshape(b, n_blk, MOBA_BLOCK, B_KV_HEADS, HEAD_DIM).mean(axis=2)
    return k, v, k_mean


def moba_attention(h, w_q, w_o, k, v, k_mean):
    b, s, _ = h.shape
    n_blk = k.shape[1] // MOBA_BLOCK
    topk = min(MOBA_TOPK, n_blk)
    scale = 1.0 / math.sqrt(HEAD_DIM)
    q = (h @ w_q).reshape(b, s, B_KV_HEADS, B_REP, HEAD_DIM)
    slopes = alibi_slopes(B_HEADS).reshape(B_KV_HEADS, B_REP)
    pos = jnp.arange(s)
    gate = jnp.einsum('bsgre,bnge->bsgrn', q.astype(jnp.float32), k_mean)
    past = jnp.arange(n_blk)[None, :] < (pos // MOBA_BLOCK)[:, None]
    gate = jnp.where(past[None, :, None, None, :], gate, -jnp.inf)
    top_val, top_idx = lax.top_k(gate, topk)
    sel_ok = jnp.isfinite(top_val)
    kb = k.reshape(b, n_blk, MOBA_BLOCK, B_KV_HEADS, HEAD_DIM).transpose(0, 3, 1, 2, 4)
    vb = v.reshape(b, n_blk, MOBA_BLOCK, B_KV_HEADS, HEAD_DIM).transpose(0, 3, 1, 2, 4)
    bi = jnp.arange(b)[:, None, None, None, None]
    gi = jnp.arange(B_KV_HEADS)[None, None, :, None, None]
    offs = jnp.arange(MOBA_BLOCK)
    n_sel = topk * MOBA_BLOCK

    def chunk_fn(args):
        qc, idx, ok, t0 = args
        t = t0 + jnp.arange(MOBA_Q_CHUNK)
        k_sel = kb[bi, gi, idx]
        v_sel = vb[bi, gi, idx]
        s_sel = jnp.einsum('bcgre,bcgrike->bcgrik', qc, k_sel,
                           preferred_element_type=jnp.float32) * scale
        dist_sel = (t[None, :, None, None, None, None]
                    - (idx[..., None] * MOBA_BLOCK + offs)).astype(jnp.float32)
        s_sel = jnp.where(ok[..., None],
                          s_sel - slopes[None, None, :, :, None, None] * dist_sel, NEG)
        start = (t0 // MOBA_BLOCK) * MOBA_BLOCK
        k_own = lax.dynamic_slice_in_dim(k, start, MOBA_BLOCK, axis=1)
        v_own = lax.dynamic_slice_in_dim(v, start, MOBA_BLOCK, axis=1)
        s_own = jnp.einsum('bcgre,bkge->bcgrk', qc, k_own,
                           preferred_element_type=jnp.float32) * scale
        dist_own = t[:, None] - (start + offs)[None, :]
        s_own = jnp.where((dist_own >= 0)[None, :, None, None, :],
                          s_own - slopes[None, None, :, :, None]
                          * dist_own.astype(jnp.float32)[None, :, None, None, :], NEG)
        probs = jax.nn.softmax(
            jnp.concatenate([s_sel.reshape(s_sel.shape[:4] + (n_sel,)), s_own], axis=-1),
            axis=-1).astype(v.dtype)
        p_sel = probs[..., :n_sel].reshape(s_sel.shape)
        p_own = probs[..., n_sel:]
        return (jnp.einsum('bcgrik,bcgrike->bcgre', p_sel, v_sel)
                + jnp.einsum('bcgrk,bkge->bcgre', p_own, v_own))

    n_chunks = s // MOBA_Q_CHUNK

    def to_chunks(t):
        return t.reshape((b, n_chunks, MOBA_Q_CHUNK) + t.shape[2:]).swapaxes(0, 1)

    t0s = jnp.arange(n_chunks) * MOBA_Q_CHUNK
    o = lax.map(chunk_fn, (to_chunks(q), to_chunks(top_idx), to_chunks(sel_ok), t0s))
    o = o.swapaxes(0, 1).reshape(b, s, B_HEADS * HEAD_DIM)
    return o @ w_o


def hierarchical_moe(h, w_group, b_group, w_expert, b_expert, w_gate, w_up, w_down):
    b, s, d = h.shape
    n = b * s
    xt = h.reshape(n, d)
    g_logits = (xt @ w_group + b_group).astype(jnp.float32)
    g_sel = jnp.argmax(g_logits, axis=-1)
    p_group = jnp.take_along_axis(jax.nn.softmax(g_logits, axis=-1), g_sel[:, None], axis=1)
    e_logits = (jnp.einsum('nd,gde->nge', xt, w_expert) + b_expert).astype(jnp.float32)
    e_logits = jnp.take_along_axis(e_logits, g_sel[:, None, None], axis=1)[:, 0]
    top_val, top_idx = lax.top_k(e_logits, MOE_TOPK)
    wts = jax.nn.softmax(top_val, axis=-1) * p_group
    expert_id = (g_sel[:, None] * MOE_EXPERTS_PER_GROUP + top_idx).astype(jnp.int32)
    a = n * MOE_TOPK
    e_flat = expert_id.reshape(a)
    tok_flat = jnp.repeat(jnp.arange(n, dtype=jnp.int32), MOE_TOPK)
    order = jnp.argsort(e_flat)
    e_sorted = e_flat[order]
    tok_sorted = tok_flat[order]
    w_sorted = wts.reshape(a)[order]
    counts = jnp.zeros((N_EXPERTS,), jnp.int32).at[e_flat].add(1)
    starts = jnp.cumsum(counts) - counts
    padded = (counts + MOE_BLOCK - 1) // MOE_BLOCK * MOE_BLOCK
    ends_p = jnp.cumsum(padded)
    dest = (ends_p - padded)[e_sorted] + jnp.arange(a, dtype=jnp.int32) - starts[e_sorted]
    n_blocks = -(-a // MOE_BLOCK) + N_EXPERTS
    slot_tok = jnp.full((n_blocks * MOE_BLOCK,), n, jnp.int32).at[dest].set(tok_sorted)
    rows = jnp.concatenate([xt, jnp.zeros((1, d), xt.dtype)], axis=0)[slot_tok]
    rows = rows.reshape(n_blocks, MOE_BLOCK, d)
    block_start = jnp.arange(n_blocks, dtype=jnp.int32) * MOE_BLOCK
    block_expert = jnp.minimum(jnp.searchsorted(ends_p, block_start, side='right'), N_EXPERTS - 1)

    def expert_block(args):
        xb, e = args
        return (jax.nn.silu(xb @ w_gate[e]) * (xb @ w_up[e])) @ w_down[e]

    out = lax.map(expert_block, (rows, block_expert)).reshape(-1, d)
    y = jax.ops.segment_sum(out[dest] * w_sorted[:, None].astype(out.dtype), tok_sorted,
                            num_segments=n)
    return y.reshape(b, s, d)


def setup_inputs(seed: int = 0) -> dict:
    key = jax.random.key(seed)
    ks = jax.random.split(key, 24)
    D = D_MODEL
    f32 = jnp.float32

    def nrm(k, shape, scale):
        return jax.random.normal(k, shape, f32) * scale

    qkv_col_scale = jnp.ones((len(A_GROUPS), 3, A_WIDTH), f32).at[:, 2].set(BETA).reshape(-1)
    kv_width = B_KV_HEADS * HEAD_DIM
    kv_col_scale = jnp.concatenate([jnp.ones((kv_width,), f32), jnp.full((kv_width,), BETA, f32)])
    return {
        'x': nrm(ks[0], (BATCH, SEQ, D), 1.0),
        'c': nrm(ks[1], (BATCH, D), 1.0),
        'ada_down': nrm(ks[2], (DEPTH, D, ADA_RANK), D ** -0.5),
        'ada_up': nrm(ks[3], (DEPTH, ADA_RANK, 6 * D), 0.1 * ADA_RANK ** -0.5),
        'ada_bias': nrm(ks[4], (DEPTH, 6 * D), 0.01),
        'ln_gain': 1.0 + nrm(ks[5], (DEPTH, 2, D), 0.02),
        'ln_bias': nrm(ks[6], (DEPTH, 2, D), 0.02),
        'a_w_qkv': nrm(ks[7], (N_A_LAYERS, D, 9 * A_WIDTH), D ** -0.5) * qkv_col_scale,
        'a_w_o': nrm(ks[8], (N_A_LAYERS, A_WIDTH, D), BETA * A_WIDTH ** -0.5),
        'kv_ada_down': nrm(ks[9], (D, ADA_RANK), D ** -0.5),
        'kv_ada_up': nrm(ks[10], (ADA_RANK, 2 * D), 0.1 * ADA_RANK ** -0.5),
        'kv_ada_bias': nrm(ks[11], (2 * D,), 0.01),
        'kv_w': nrm(ks[12], (D, 2 * kv_width), D ** -0.5) * kv_col_scale,
        'b_w_q': nrm(ks[13], (N_B_LAYERS, D, B_HEADS * HEAD_DIM), D ** -0.5),
        'b_w_o': nrm(ks[14], (N_B_LAYERS, B_HEADS * HEAD_DIM, D), BETA * (B_HEADS * HEAD_DIM) ** -0.5),
        'moe_w_group': nrm(ks[15], (DEPTH, D, MOE_GROUPS), D ** -0.5),
        'moe_b_group': nrm(ks[16], (DEPTH, MOE_GROUPS), 0.01),
        'moe_w_expert': nrm(ks[17], (DEPTH, MOE_GROUPS, D, MOE_EXPERTS_PER_GROUP), D ** -0.5),
        'moe_b_expert': nrm(ks[18], (DEPTH, MOE_GROUPS, MOE_EXPERTS_PER_GROUP), 0.01),
        'moe_w_gate': nrm(ks[19], (DEPTH, N_EXPERTS, D, D_EXPERT), D ** -0.5),
        'moe_w_up': nrm(ks[20], (DEPTH, N_EXPERTS, D, D_EXPERT), D ** -0.5),
        'moe_w_down': nrm(ks[21], (DEPTH, N_EXPERTS, D_EXPERT, D), BETA * D_EXPERT ** -0.5),
    }


def reference(x, c, ada_down, ada_up, ada_bias, ln_gain, ln_bias, a_w_qkv, a_w_o,
              kv_ada_down, kv_ada_up, kv_ada_bias, kv_w, b_w_q, b_w_o,
              moe_w_group, moe_b_group, moe_w_expert, moe_b_expert,
              moe_w_gate, moe_w_up, moe_w_down):
    for l in range(DEPTH):
        shift1, scale1, gate1, shift2, scale2, gate2 = modulation(
            c, ada_down[l], ada_up[l], ada_bias[l], 6)
        hm = x * (1 + scale1) + shift1
        if l < N_A_LAYERS:
            mix = dilated_mixture_attention(hm, a_w_qkv[l], a_w_o[l])
        else:
            if l == N_A_LAYERS:
                k_sh, v_sh, k_mean = shared_kv(x, c, kv_ada_down, kv_ada_up, kv_ada_bias, kv_w)
            j = l - N_A_LAYERS
            mix = moba_attention(hm, b_w_q[j], b_w_o[j], k_sh, v_sh, k_mean)
        x = layer_norm(ALPHA * x + (1 + gate1) * mix, ln_gain[l, 0], ln_bias[l, 0])
        hf = x * (1 + scale2) + shift2
        ffn = hierarchical_moe(hf, moe_w_group[l], moe_b_group[l], moe_w_expert[l],
                               moe_b_expert[l], moe_w_gate[l], moe_w_up[l], moe_w_down[l])
        x = layer_norm(ALPHA * x + (1 + gate2) * ffn, ln_gain[l, 1], ln_bias[l, 1])
    return x
```

```python
import functools
import math

import numpy as np
import jax
import jax.numpy as jnp
from jax import lax
from jax.experimental import pallas as pl
from jax.experimental.pallas import tpu as pltpu

F32 = jnp.float32
BF16 = jnp.bfloat16
HIGHEST = lax.Precision.HIGHEST

HEAD_DIM = 128
A_GROUPS = ((128, 1), (512, 4), (2048, 16))
A_BLOCK = 128
MOBA_BLOCK = 256
MOBA_TOPK = 3
MOBA_REP = 4
MOE_TOPK = 2
MOE_BLOCK = 128
LN_EPS = 1e-5
NEG = -1e30
LANES = 128
VMEM_LIMIT = 56 * 1024 * 1024

_NT = (((1,), (1,)), ((), ()))


def _params(sem, vmem=VMEM_LIMIT):
    return pltpu.CompilerParams(dimension_semantics=sem, vmem_limit_bytes=vmem)


def _mod_kernel(c_ref, wd_ref, wu_ref, b_ref, o_ref, t_ref):
    @pl.when(pl.program_id(1) == 0)
    def _():
        c = c_ref[...]
        sc = c * (1.0 / (1.0 + jnp.exp(-c)))
        t_ref[...] = jnp.dot(sc, wd_ref[...], preferred_element_type=F32, precision=HIGHEST)

    o_ref[...] = jnp.dot(t_ref[...], wu_ref[...], preferred_element_type=F32,
                         precision=HIGHEST) + b_ref[...]


def _modulation(c, w_down, w_up, b_up):
    n_l, d, r = w_down.shape
    n = w_up.shape[-1]
    tn = n // pl.cdiv(n, 6144)
    c8 = jnp.broadcast_to(c, (8, d))
    out = pl.pallas_call(
        _mod_kernel,
        out_shape=jax.ShapeDtypeStruct((n_l, 8, n), F32),
        grid=(n_l, n // tn),
        in_specs=[
            pl.BlockSpec((8, d), lambda l, j: (0, 0)),
            pl.BlockSpec((None, d, r), lambda l, j: (l, 0, 0)),
            pl.BlockSpec((None, r, tn), lambda l, j: (l, 0, j)),
            pl.BlockSpec((None, 1, tn), lambda l, j: (l, 0, j)),
        ],
        out_specs=pl.BlockSpec((None, 8, tn), lambda l, j: (l, 0, j)),
        scratch_shapes=[pltpu.VMEM((8, r), F32)],
        compiler_params=_params(("arbitrary", "arbitrary")),
        name="modulation",
    )(c8, w_down, w_up, b_up.reshape(n_l, 1, n))
    return out[:, 0, :]


def _mm_kernel(a_ref, b_ref, o_ref):
    o_ref[...] = jnp.dot(a_ref[...], b_ref[...],
                         preferred_element_type=F32).astype(o_ref.dtype)


def _matmul(a, b, out_dtype, n_off=0, n_out=None, dil=1, tm=512, tn=1024):
    m, k = a.shape
    n_out = b.shape[1] if n_out is None else n_out
    tm = min(tm, m // dil)
    tn = min(tn, n_out)
    off = n_off // tn
    bps = m // dil // tm
    return pl.pallas_call(
        _mm_kernel,
        out_shape=jax.ShapeDtypeStruct((m, n_out), out_dtype),
        grid=(n_out // tn, m // tm),
        in_specs=[
            pl.BlockSpec((tm, k), lambda j, i: (i % bps, i // bps)),
            pl.BlockSpec((k, tn), lambda j, i: (0, j + off)),
        ],
        out_specs=pl.BlockSpec((tm, tn), lambda j, i: (i, j)),
        compiler_params=_params(("parallel", "parallel")),
        name="matmul",
    )(a.reshape(m // dil, dil * k), b)


def _layer_norm(v, gain, bias):
    mu = jnp.mean(v, axis=-1, keepdims=True)
    vc = v - mu
    var = jnp.mean(vc * vc, axis=-1, keepdims=True)
    return vc * lax.rsqrt(var + LN_EPS) * gain + bias


_V_GATE, _V_GAIN, _V_BIAS, _V_SCALE_A, _V_SHIFT_A, _V_SCALE_B, _V_SHIFT_B = range(7)


def _prologue_kernel(x_ref, vec_ref, o_ref):
    h = x_ref[...] * (1.0 + vec_ref[_V_SCALE_A:_V_SCALE_A + 1, :]) \
        + vec_ref[_V_SHIFT_A:_V_SHIFT_A + 1, :]
    o_ref[...] = h.astype(BF16)


def _prologue(x2, vecs, tm=256):
    s, d = x2.shape
    return pl.pallas_call(
        _prologue_kernel,
        out_shape=jax.ShapeDtypeStruct((s, d), BF16),
        grid=(s // tm,),
        in_specs=[pl.BlockSpec((tm, d), lambda i: (i, 0)),
                  pl.BlockSpec((8, d), lambda i: (0, 0))],
        out_specs=pl.BlockSpec((tm, d), lambda i: (i, 0)),
        compiler_params=_params(("parallel",)),
        name="prologue",
    )(x2, vecs)


def _ln_router_kernel(alpha, n_groups, n_per_group,
                      x_ref, mix_ref, vec_ref, wr_ref, br_ref, xo_ref, route_ref):
    gate = vec_ref[_V_GATE:_V_GATE + 1, :]
    x1 = alpha * x_ref[...] + (1.0 + gate) * mix_ref[...]
    xn = _layer_norm(x1, vec_ref[_V_GAIN:_V_GAIN + 1, :], vec_ref[_V_BIAS:_V_BIAS + 1, :])
    xo_ref[...] = xn
    hf = xn * (1.0 + vec_ref[_V_SCALE_A:_V_SCALE_A + 1, :]) + vec_ref[_V_SHIFT_A:_V_SHIFT_A + 1, :]
    logits = jnp.dot(hf, wr_ref[...], preferred_element_type=F32, precision=HIGHEST) + br_ref[...]
    lane = lax.broadcasted_iota(jnp.int32, logits.shape, 1).astype(F32)
    ninf = -jnp.inf
    big = float(LANES)
    is_g = lane < n_groups
    gl = jnp.where(is_g, logits, ninf)
    gmax = jnp.max(gl, axis=-1, keepdims=True)
    gsel = jnp.min(jnp.where(gl == gmax, lane, big), axis=-1, keepdims=True)
    p_group = 1.0 / jnp.sum(jnp.where(is_g, jnp.exp(logits - gmax), 0.0), axis=-1, keepdims=True)
    lo = n_groups + gsel * n_per_group
    el = jnp.where((lane >= lo) & (lane < lo + n_per_group), logits, ninf)
    m1 = jnp.max(el, axis=-1, keepdims=True)
    i1 = jnp.min(jnp.where(el == m1, lane, big), axis=-1, keepdims=True)
    el2 = jnp.where(lane == i1, ninf, el)
    m2 = jnp.max(el2, axis=-1, keepdims=True)
    i2 = jnp.min(jnp.where(el2 == m2, lane, big), axis=-1, keepdims=True)
    e21 = jnp.exp(m2 - m1)
    inv = 1.0 / (1.0 + e21)
    w1 = p_group * inv
    w2 = p_group * (e21 * inv)
    route_ref[...] = jnp.where(lane == 0.0, i1 - n_groups,
                     jnp.where(lane == 1.0, i2 - n_groups,
                     jnp.where(lane == 2.0, w1,
                     jnp.where(lane == 3.0, w2, 0.0))))


def _ln_router(x2, mix, vecs, wr, br, alpha, n_groups, n_per_group, tm=256):
    s, d = x2.shape
    return pl.pallas_call(
        functools.partial(_ln_router_kernel, alpha, n_groups, n_per_group),
        out_shape=[jax.ShapeDtypeStruct((s, d), F32),
                   jax.ShapeDtypeStruct((s, LANES), F32)],
        grid=(s // tm,),
        in_specs=[pl.BlockSpec((tm, d), lambda i: (i, 0)),
                  pl.BlockSpec((tm, d), lambda i: (i, 0)),
                  pl.BlockSpec((8, d), lambda i: (0, 0)),
                  pl.BlockSpec((d, LANES), lambda i: (0, 0)),
                  pl.BlockSpec((1, LANES), lambda i: (0, 0))],
        out_specs=[pl.BlockSpec((tm, d), lambda i: (i, 0)),
                   pl.BlockSpec((tm, LANES), lambda i: (i, 0))],
        compiler_params=_params(("parallel",)),
        name="ln_router",
    )(x2, mix, vecs, wr, br)


def _ln_combine_kernel(alpha, with_next, with_kv, x_ref, y_ref, route_ref, vec_ref, *refs):
    d = x_ref.shape[1]
    w1 = route_ref[:, 2:3]
    w2 = route_ref[:, 3:4]
    ffn = y_ref[:, :d] * w1 + y_ref[:, d:] * w2
    gate = vec_ref[_V_GATE:_V_GATE + 1, :]
    x1 = alpha * x_ref[...] + (1.0 + gate) * ffn
    xn = _layer_norm(x1, vec_ref[_V_GAIN:_V_GAIN + 1, :], vec_ref[_V_BIAS:_V_BIAS + 1, :])
    refs[0][...] = xn
    k = 1
    if with_kv:
        hk = xn * (1.0 + vec_ref[_V_SCALE_B:_V_SCALE_B + 1, :]) + vec_ref[_V_SHIFT_B:_V_SHIFT_B + 1, :]
        refs[k][...] = hk.astype(BF16)
        k += 1
    if with_next:
        h = xn * (1.0 + vec_ref[_V_SCALE_A:_V_SCALE_A + 1, :]) + vec_ref[_V_SHIFT_A:_V_SHIFT_A + 1, :]
        refs[k][...] = h.astype(BF16)


def _ln_combine(x2, y2, route, vecs, alpha, with_next, with_kv, tm=256):
    s, d = x2.shape
    row = lambda i: (i, 0)
    n_bf = int(with_kv) + int(with_next)
    shapes = [jax.ShapeDtypeStruct((s, d), F32)] + [jax.ShapeDtypeStruct((s, d), BF16)] * n_bf
    specs = [pl.BlockSpec((tm, d), row)] * (1 + n_bf)
    return pl.pallas_call(
        functools.partial(_ln_combine_kernel, alpha, with_next, with_kv),
        out_shape=shapes,
        grid=(s // tm,),
        in_specs=[pl.BlockSpec((tm, d), row),
                  pl.BlockSpec((tm, 2 * d), row),
                  pl.BlockSpec((tm, LANES), row),
                  pl.BlockSpec((8, d), lambda i: (0, 0))],
        out_specs=specs,
        compiler_params=_params(("parallel",)),
        name="ln_combine",
    )(x2, y2, route, vecs)


def _dilated_kernel(dil, blocks_per_stream, slopes, q_ref, kp_ref, kc_ref, vp_ref, vc_ref,
                    o_ref, lse_ref):
    nb = pl.program_id(0) % blocks_per_stream
    has_prev = nb > 0
    qi = lax.broadcasted_iota(jnp.int32, (A_BLOCK, A_BLOCK), 0)
    kj = lax.broadcasted_iota(jnp.int32, (A_BLOCK, A_BLOCK), 1)
    ok_prev = (kj >= qi) & has_prev
    ok_cur = kj <= qi
    dist_prev = ((qi + A_BLOCK - kj) * dil).astype(F32)
    dist_cur = ((qi - kj) * dil).astype(F32)
    inv_sqrt = 1.0 / math.sqrt(HEAD_DIM)
    lane = lax.broadcasted_iota(jnp.int32, (A_BLOCK, LANES), 1)
    lse_all = jnp.zeros((A_BLOCK, LANES), F32)
    for h, slope in enumerate(slopes):
        hs = slice(h * HEAD_DIM, (h + 1) * HEAD_DIM)
        q = q_ref[:, hs]
        s_p = lax.dot_general(q, kp_ref[:, hs], _NT, preferred_element_type=F32) * inv_sqrt
        s_c = lax.dot_general(q, kc_ref[:, hs], _NT, preferred_element_type=F32) * inv_sqrt
        s_p = jnp.where(ok_prev, s_p - slope * dist_prev, NEG)
        s_c = jnp.where(ok_cur, s_c - slope * dist_cur, NEG)
        m = jnp.maximum(jnp.max(s_p, axis=-1, keepdims=True), jnp.max(s_c, axis=-1, keepdims=True))
        p_p = jnp.exp(s_p - m)
        p_c = jnp.exp(s_c - m)
        l = jnp.sum(p_p, axis=-1, keepdims=True) + jnp.sum(p_c, axis=-1, keepdims=True)
        o = jnp.dot(p_p.astype(BF16), vp_ref[:, hs], preferred_element_type=F32) \
            + jnp.dot(p_c.astype(BF16), vc_ref[:, hs], preferred_element_type=F32)
        o_ref[:, hs] = o * (1.0 / l)
        lse_all = jnp.where(lane == h, m + jnp.log(l), lse_all)
    lse_ref[...] = lse_all


def _dilated_attention(qkv, dil, n_heads):
    s = qkv.shape[0]
    w = n_heads * HEAD_DIM
    n = s // dil
    bps = n // A_BLOCK
    slopes = [float(v) for v in np.exp2(-8.0 * np.arange(1, n_heads + 1, dtype=np.float32) / n_heads)]
    prev = lambda b: (jnp.maximum(b - 1, 0), 1)
    o, lse = pl.pallas_call(
        functools.partial(_dilated_kernel, dil, bps, slopes),
        out_shape=[jax.ShapeDtypeStruct((n, dil * w), F32),
                   jax.ShapeDtypeStruct((n, dil * LANES), F32)],
        grid=(s // A_BLOCK,),
        in_specs=[pl.BlockSpec((A_BLOCK, w), lambda b: (b, 0)),
                  pl.BlockSpec((A_BLOCK, w), prev),
                  pl.BlockSpec((A_BLOCK, w), lambda b: (b, 1)),
                  pl.BlockSpec((A_BLOCK, w), lambda b: (jnp.maximum(b - 1, 0), 2)),
                  pl.BlockSpec((A_BLOCK, w), lambda b: (b, 2))],
        out_specs=[pl.BlockSpec((A_BLOCK, w), lambda b: (b % bps, b // bps)),
                   pl.BlockSpec((A_BLOCK, LANES), lambda b: (b % bps, b // bps))],
        compiler_params=_params(("parallel",)),
        name="dilated_attention",
    )(qkv, qkv, qkv, qkv, qkv)
    return o.reshape(s, w), lse.reshape(s, LANES)


def _merge_kernel(n_heads, o0_ref, o1_ref, o2_ref, l0_ref, l1_ref, l2_ref, out_ref):
    l0, l1, l2 = l0_ref[...], l1_ref[...], l2_ref[...]
    m = jnp.maximum(jnp.maximum(l0, l1), l2)
    e0, e1, e2 = jnp.exp(l0 - m), jnp.exp(l1 - m), jnp.exp(l2 - m)
    inv = 1.0 / (e0 + e1 + e2)
    w0, w1, w2 = e0 * inv, e1 * inv, e2 * inv
    for h in range(n_heads):
        hs = slice(h * HEAD_DIM, (h + 1) * HEAD_DIM)
        acc = o0_ref[:, hs] * w0[:, h:h + 1] + o1_ref[:, hs] * w1[:, h:h + 1] \
            + o2_ref[:, hs] * w2[:, h:h + 1]
        out_ref[:, hs] = acc.astype(BF16)


def _merge_groups(os_, lses, n_heads, tm=256):
    s, w = os_[0].shape
    row = lambda i: (i, 0)
    return pl.pallas_call(
        functools.partial(_merge_kernel, n_heads),
        out_shape=jax.ShapeDtypeStruct((s, w), BF16),
        grid=(s // tm,),
        in_specs=[pl.BlockSpec((tm, w), row)] * 3 + [pl.BlockSpec((tm, LANES), row)] * 3,
        out_specs=pl.BlockSpec((tm, w), row),
        compiler_params=_params(("parallel",)),
        name="merge_groups",
    )(*os_, *lses)


def _kmean_kernel(k_ref, o_ref):
    o_ref[...] = jnp.mean(k_ref[...], axis=0, keepdims=True)


def _block_means(k):
    s, kw = k.shape
    nb = s // MOBA_BLOCK
    return pl.pallas_call(
        _kmean_kernel,
        out_shape=jax.ShapeDtypeStruct((nb, 1, kw), F32),
        grid=(nb,),
        in_specs=[pl.BlockSpec((MOBA_BLOCK, kw), lambda i: (i, 0))],
        out_specs=pl.BlockSpec((None, 1, kw), lambda i: (i, 0, 0)),
        compiler_params=_params(("parallel",)),
        name="block_means",
    )(k)


def _moba_kernel(q_ref, k_ref, vt_ref, kmean_ref, slope_ref, o_ref,
                 sel_ref, m_ref, l_ref, acc_ref):
    qb = pl.program_id(1)
    n_blk = kmean_ref.shape[0]
    rows = MOBA_REP * MOBA_BLOCK
    q = q_ref[...]
    q4 = jnp.concatenate([q[:, r * HEAD_DIM:(r + 1) * HEAD_DIM] for r in range(MOBA_REP)], axis=0)
    gate = lax.dot_general(kmean_ref[...], q4, _NT, preferred_element_type=F32, precision=HIGHEST)
    blk = lax.broadcasted_iota(jnp.int32, (n_blk, rows), 0).astype(F32)
    ninf = -jnp.inf
    gate = jnp.where(blk < qb.astype(F32), gate, ninf)
    sel = jnp.zeros((n_blk, rows), F32)
    for _ in range(MOBA_TOPK):
        top = jnp.max(gate, axis=0, keepdims=True)
        idx = jnp.min(jnp.where(gate == top, blk, float(n_blk)), axis=0, keepdims=True)
        hit = (blk == idx) & (top > ninf)
        sel = jnp.where(hit, 1.0, sel)
        gate = jnp.where(hit, ninf, gate)
    sel_ref[...] = sel

    qbf = q4.astype(BF16)
    scale = 1.0 / math.sqrt(HEAD_DIM)
    slope = slope_ref[...]
    lane_t = lax.broadcasted_iota(jnp.int32, (1, rows), 1) % MOBA_BLOCK
    t_pos = (qb * MOBA_BLOCK + lane_t).astype(F32)
    sub = lax.broadcasted_iota(jnp.int32, (MOBA_BLOCK, 1), 0)

    def scores(n):
        s_t = lax.dot_general(k_ref[n], qbf, _NT, preferred_element_type=F32) * scale
        dist = t_pos - (n * MOBA_BLOCK + sub).astype(F32)
        return s_t - slope * dist, dist

    s_t, dist = scores(qb)
    s_t = jnp.where(dist >= 0.0, s_t, NEG)
    m0 = jnp.max(s_t, axis=0, keepdims=True)
    p = jnp.exp(s_t - m0)
    m_ref[...] = m0
    l_ref[...] = jnp.sum(p, axis=0, keepdims=True)
    acc_ref[...] = jnp.dot(vt_ref[qb], p.astype(BF16), preferred_element_type=F32)

    def body(n, carry):
        s_n, _ = scores(n)
        s_n = jnp.where(sel_ref[pl.ds(n, 1), :] > 0.0, s_n, NEG)
        m_old = m_ref[...]
        m_new = jnp.maximum(m_old, jnp.max(s_n, axis=0, keepdims=True))
        alpha = jnp.exp(m_old - m_new)
        p_n = jnp.exp(s_n - m_new)
        m_ref[...] = m_new
        l_ref[...] = alpha * l_ref[...] + jnp.sum(p_n, axis=0, keepdims=True)
        acc_ref[...] = alpha * acc_ref[...] + jnp.dot(vt_ref[n], p_n.astype(BF16),
                                                      preferred_element_type=F32)
        return carry

    lax.fori_loop(0, qb, body, 0)
    o_t = acc_ref[...] * (1.0 / l_ref[...])
    for r in range(MOBA_REP):
        o_ref[:, r * HEAD_DIM:(r + 1) * HEAD_DIM] = \
            o_t[:, r * MOBA_BLOCK:(r + 1) * MOBA_BLOCK].T.astype(BF16)


def _moba_attention(q, k_blocks, vt_blocks, k_mean, slope_lanes):
    s = q.shape[0]
    g, nb = k_blocks.shape[:2]
    rows = MOBA_REP * MOBA_BLOCK
    qw = MOBA_REP * HEAD_DIM
    return pl.pallas_call(
        _moba_kernel,
        out_shape=jax.ShapeDtypeStruct((s, g * qw), BF16),
        grid=(g, nb),
        in_specs=[pl.BlockSpec((MOBA_BLOCK, qw), lambda gi, qb: (qb, gi)),
                  pl.BlockSpec((None, nb, MOBA_BLOCK, HEAD_DIM), lambda gi, qb: (gi, 0, 0, 0)),
                  pl.BlockSpec((None, nb, HEAD_DIM, MOBA_BLOCK), lambda gi, qb: (gi, 0, 0, 0)),
                  pl.BlockSpec((None, nb, HEAD_DIM), lambda gi, qb: (gi, 0, 0)),
                  pl.BlockSpec((None, 1, rows), lambda gi, qb: (gi, 0, 0))],
        out_specs=pl.BlockSpec((MOBA_BLOCK, qw), lambda gi, qb: (qb, gi)),
        scratch_shapes=[pltpu.VMEM((nb, rows), F32),
                        pltpu.VMEM((1, rows), F32),
                        pltpu.VMEM((1, rows), F32),
                        pltpu.VMEM((HEAD_DIM, rows), F32)],
        compiler_params=_params(("parallel", "parallel")),
        name="moba_attention",
    )(q, k_blocks, vt_blocks, k_mean, slope_lanes)


def _moe_kernel(blk_e_ref, blk_n_ref, blk_first_ref, slot_tok_ref, slot_dst_ref,
                x_hbm, vec_ref, wg_ref, wu_ref, wd_ref, y_hbm,
                xbuf, obuf, wg_bf, wu_bf, wd_bf, gsem, ssem):
    b = pl.program_id(0)
    nv = blk_n_ref[b]
    base = b * MOE_BLOCK

    @pl.when(nv > 0)
    def _():
        def gather(j, c):
            tok = slot_tok_ref[base + j]
            pltpu.make_async_copy(x_hbm.at[pl.ds(tok, 1)], xbuf.at[pl.ds(j, 1)], gsem).start()
            return c
        lax.fori_loop(0, MOE_BLOCK, gather, 0)

        @pl.when(blk_first_ref[b] == 1)
        def _():
            wg_bf[...] = wg_ref[...].astype(BF16)
            wu_bf[...] = wu_ref[...].astype(BF16)
            wd_bf[...] = wd_ref[...].astype(BF16)

        pltpu.make_async_copy(x_hbm.at[pl.ds(0, MOE_BLOCK)], xbuf, gsem).wait()
        hf = xbuf[...] * (1.0 + vec_ref[_V_SCALE_A:_V_SCALE_A + 1, :]) \
            + vec_ref[_V_SHIFT_A:_V_SHIFT_A + 1, :]
        hb = hf.astype(BF16)
        gt = jnp.dot(hb, wg_bf[...], preferred_element_type=F32)
        up = jnp.dot(hb, wu_bf[...], preferred_element_type=F32)
        hid = gt * (1.0 / (1.0 + jnp.exp(-gt))) * up
        obuf[...] = jnp.dot(hid.astype(BF16), wd_bf[...], preferred_element_type=F32)

        def scatter(j, c):
            dst = slot_dst_ref[base + j]
            pltpu.make_async_copy(obuf.at[pl.ds(j, 1)], y_hbm.at[pl.ds(dst, 1)], ssem).start()
            return c
        lax.fori_loop(0, nv, scatter, 0)

        def drain(j, c):
            pltpu.make_async_copy(obuf.at[pl.ds(0, 1)], y_hbm.at[pl.ds(0, 1)], ssem).wait()
            return c
        lax.fori_loop(0, nv, drain, 0)


def _moe_dispatch(ids, n_experts):
    s, k = ids.shape
    a = s * k
    n_blocks = -(-a // MOE_BLOCK) + n_experts
    e_flat = ids.reshape(a)
    order = jnp.argsort(e_flat, stable=True).astype(jnp.int32)
    e_sorted = e_flat[order]
    counts = jnp.zeros((n_experts,), jnp.int32).at[e_flat].add(1)
    starts = jnp.cumsum(counts) - counts
    nblk = (counts + MOE_BLOCK - 1) // MOE_BLOCK
    blk_end = jnp.cumsum(nblk)
    blk_start = blk_end - nblk
    dest = blk_start[e_sorted] * MOE_BLOCK + jnp.arange(a, dtype=jnp.int32) - starts[e_sorted]
    slot_tok = jnp.zeros((n_blocks * MOE_BLOCK,), jnp.int32).at[dest].set(order // k)
    slot_dst = jnp.zeros((n_blocks * MOE_BLOCK,), jnp.int32).at[dest].set(order)
    bidx = jnp.arange(n_blocks, dtype=jnp.int32)
    total = blk_end[-1]
    used = bidx < total
    blk_e = jnp.searchsorted(blk_end, jnp.minimum(bidx, total - 1), side="right").astype(jnp.int32)
    blk_e = jnp.minimum(blk_e, n_experts - 1)
    blk_n = jnp.where(used, jnp.clip(counts[blk_e] - (bidx - blk_start[blk_e]) * MOE_BLOCK,
                                     0, MOE_BLOCK), 0).astype(jnp.int32)
    blk_first = (used & (bidx == blk_start[blk_e])).astype(jnp.int32)
    return blk_e, blk_n, blk_first, slot_tok, slot_dst, n_blocks


def _moe_experts(x2, vecs, ids, w_gate, w_up, w_down):
    s, d = x2.shape
    n_experts, _, f = w_gate.shape
    k = ids.shape[1]
    blk_e, blk_n, blk_first, slot_tok, slot_dst, n_blocks = _moe_dispatch(ids, n_experts)
    y = pl.pallas_call(
        _moe_kernel,
        out_shape=jax.ShapeDtypeStruct((s * k, d), F32),
        grid_spec=pltpu.PrefetchScalarGridSpec(
            num_scalar_prefetch=5,
            grid=(n_blocks,),
            in_specs=[pl.BlockSpec(memory_space=pl.ANY),
                      pl.BlockSpec((8, d), lambda b, *_: (0, 0)),
                      pl.BlockSpec((None, d, f), lambda b, be, *_: (be[b], 0, 0)),
                      pl.BlockSpec((None, d, f), lambda b, be, *_: (be[b], 0, 0)),
                      pl.BlockSpec((None, f, d), lambda b, be, *_: (be[b], 0, 0))],
            out_specs=pl.BlockSpec(memory_space=pl.ANY),
            scratch_shapes=[pltpu.VMEM((MOE_BLOCK, d), F32),
                            pltpu.VMEM((MOE_BLOCK, d), F32),
                            pltpu.VMEM((d, f), BF16),
                            pltpu.VMEM((d, f), BF16),
                            pltpu.VMEM((f, d), BF16),
                            pltpu.SemaphoreType.DMA(()),
                            pltpu.SemaphoreType.DMA(())]),
        compiler_params=_params(("arbitrary",)),
        name="moe_experts",
    )(blk_e, blk_n, blk_first, slot_tok, slot_dst, x2, vecs, w_gate, w_up, w_down)
    return y.reshape(s, k * d)


def _vec_pack(d, **rows):
    out = jnp.zeros((8, d), F32)
    for name, v in rows.items():
        out = out.at[name_to_row[name]].set(v)
    return out


name_to_row = {"gate": _V_GATE, "gain": _V_GAIN, "bias": _V_BIAS, "scale_a": _V_SCALE_A,
               "shift_a": _V_SHIFT_A, "scale_b": _V_SCALE_B, "shift_b": _V_SHIFT_B}


def kernel(x, c, ada_down, ada_up, ada_bias, ln_gain, ln_bias, a_w_qkv, a_w_o, kv_ada_down, kv_ada_up, kv_ada_bias, kv_w, b_w_q, b_w_o, moe_w_group, moe_b_group, moe_w_expert, moe_b_expert, moe_w_gate, moe_w_up, moe_w_down):
    batch, s, d = x.shape
    assert batch == 1
    depth = ada_down.shape[0]
    n_a = a_w_qkv.shape[0]
    alpha = (2 * depth) ** 0.25
    a_heads = a_w_o.shape[1] // HEAD_DIM
    a_width = a_heads * HEAD_DIM
    n_groups = moe_w_group.shape[-1]
    n_per_group = moe_w_expert.shape[-1]
    kv_heads = kv_w.shape[1] // (2 * HEAD_DIM)
    kv_width = kv_heads * HEAD_DIM
    b_heads = kv_heads * MOBA_REP
    n_blk = s // MOBA_BLOCK

    mods = _modulation(c, ada_down, ada_up, ada_bias).reshape(depth, 6, d)
    kv_mod = _modulation(c, kv_ada_down[None], kv_ada_up[None], kv_ada_bias[None]).reshape(2, d)

    x2 = x.reshape(s, d)
    hs = _prologue(x2, _vec_pack(d, scale_a=mods[0, 1], shift_a=mods[0, 0]))

    k_blocks = vt_blocks = k_mean = slope_lanes = None
    for l in range(depth):
        shift1, scale1, gate1, shift2, scale2, gate2 = [mods[l, i] for i in range(6)]
        if l < n_a:
            w_qkv = a_w_qkv[l].astype(BF16)
            outs, lses = [], []
            for g, (_, dil) in enumerate(A_GROUPS):
                qkv = _matmul(hs, w_qkv, BF16, n_off=g * 3 * a_width, n_out=3 * a_width, dil=dil)
                o_g, lse_g = _dilated_attention(qkv, dil, a_heads)
                outs.append(o_g)
                lses.append(lse_g)
            merged = _merge_groups(outs, lses, a_heads)
            mix = _matmul(merged, a_w_o[l].astype(BF16), F32)
        else:
            j = l - n_a
            q = _matmul(hs, b_w_q[j].astype(BF16), F32)
            o = _moba_attention(q, k_blocks, vt_blocks, k_mean, slope_lanes)
            mix = _matmul(o, b_w_o[j].astype(BF16), F32)

        wr = jnp.concatenate(
            [moe_w_group[l], jnp.moveaxis(moe_w_expert[l], 0, 1).reshape(d, n_groups * n_per_group),
             jnp.zeros((d, LANES - n_groups * (1 + n_per_group)), F32)], axis=1)
        br = jnp.concatenate(
            [moe_b_group[l], moe_b_expert[l].reshape(-1),
             jnp.zeros((LANES - n_groups * (1 + n_per_group),), F32)]).reshape(1, LANES)
        vec1 = _vec_pack(d, gate=gate1, gain=ln_gain[l, 0], bias=ln_bias[l, 0],
                         scale_a=scale2, shift_a=shift2)
        x2, route = _ln_router(x2, mix, vec1, wr, br, alpha, n_groups, n_per_group)
        ids = route[:, :MOE_TOPK].astype(jnp.int32)
        y2 = _moe_experts(x2, vec1, ids, moe_w_gate[l], moe_w_up[l], moe_w_down[l])

        last = l == depth - 1
        with_kv = l == n_a - 1 and n_a < depth
        rows = dict(gate=gate2, gain=ln_gain[l, 1], bias=ln_bias[l, 1])
        if not last:
            rows.update(scale_a=mods[l + 1, 1], shift_a=mods[l + 1, 0])
        if with_kv:
            rows.update(scale_b=kv_mod[1], shift_b=kv_mod[0])
        res = _ln_combine(x2, y2, route, _vec_pack(d, **rows), alpha, not last, with_kv)
        x2 = res[0]
        if with_kv:
            kv = _matmul(res[1], kv_w.astype(BF16), F32)
            k_f32 = kv[:, :kv_width]
            k_mean = _block_means(k_f32).reshape(n_blk, kv_heads, HEAD_DIM).transpose(1, 0, 2)
            k_blocks = k_f32.astype(BF16).reshape(n_blk, MOBA_BLOCK, kv_heads, HEAD_DIM) \
                .transpose(2, 0, 1, 3)
            vt_blocks = kv[:, kv_width:].astype(BF16).reshape(n_blk, MOBA_BLOCK, kv_heads, HEAD_DIM) \
                .transpose(2, 0, 3, 1)
            slopes = jnp.exp2(-8.0 * jnp.arange(1, b_heads + 1, dtype=F32) / b_heads)
            slope_lanes = jnp.repeat(slopes.reshape(kv_heads, MOBA_REP), MOBA_BLOCK, axis=1) \
                .reshape(kv_heads, 1, MOBA_REP * MOBA_BLOCK)
        if not last:
            hs = res[-1]
    return x2.reshape(batch, s, d)
```

```python
import functools
import math

import numpy as np
import jax
import jax.numpy as jnp
from jax import lax
from jax.experimental import pallas as pl
from jax.experimental.pallas import tpu as pltpu

F32 = jnp.float32
BF16 = jnp.bfloat16
HIGHEST = lax.Precision.HIGHEST

HEAD_DIM = 128
A_GROUPS = ((128, 1), (512, 4), (2048, 16))
A_BLOCK = 128
MOBA_BLOCK = 256
MOBA_TOPK = 3
MOBA_REP = 4
MOBA_UNROLL = 4
MOE_TOPK = 2
MOE_BLOCK = 128
LN_EPS = 1e-5
NEG = -1e30
LOG2E = math.log2(math.e)
LANES = 128
VMEM_LIMIT = 56 * 1024 * 1024

_NT = (((1,), (1,)), ((), ()))


def _params(sem, vmem=VMEM_LIMIT):
    return pltpu.CompilerParams(dimension_semantics=sem, vmem_limit_bytes=vmem)


def _mod_kernel(c_ref, wd_ref, wu_ref, b_ref, o_ref, t_ref):
    @pl.when(pl.program_id(1) == 0)
    def _():
        c = c_ref[...]
        sc = c * (1.0 / (1.0 + jnp.exp(-c)))
        t_ref[...] = jnp.dot(sc, wd_ref[...], preferred_element_type=F32, precision=HIGHEST)

    o_ref[...] = jnp.dot(t_ref[...], wu_ref[...], preferred_element_type=F32,
                         precision=HIGHEST) + b_ref[...]


def _modulation(c, w_down, w_up, b_up):
    n_l, d, r = w_down.shape
    n = w_up.shape[-1]
    tn = n // pl.cdiv(n, 6144)
    c8 = jnp.broadcast_to(c, (8, d))
    out = pl.pallas_call(
        _mod_kernel,
        out_shape=jax.ShapeDtypeStruct((n_l, 8, n), F32),
        grid=(n_l, n // tn),
        in_specs=[
            pl.BlockSpec((8, d), lambda l, j: (0, 0)),
            pl.BlockSpec((None, d, r), lambda l, j: (l, 0, 0)),
            pl.BlockSpec((None, r, tn), lambda l, j: (l, 0, j)),
            pl.BlockSpec((None, 1, tn), lambda l, j: (l, 0, j)),
        ],
        out_specs=pl.BlockSpec((None, 8, tn), lambda l, j: (l, 0, j)),
        scratch_shapes=[pltpu.VMEM((8, r), F32)],
        compiler_params=_params(("arbitrary", "arbitrary")),
        name="modulation",
    )(c8, w_down, w_up, b_up.reshape(n_l, 1, n))
    return out[:, 0, :]


def _mm_kernel(a_ref, b_ref, o_ref):
    o_ref[...] = jnp.dot(a_ref[...], b_ref[...],
                         preferred_element_type=F32).astype(o_ref.dtype)


def _matmul(a, b, out_dtype, n_off=0, n_out=None, dil=1, tm=512, tn=1024):
    m, k = a.shape
    n_out = b.shape[1] if n_out is None else n_out
    tm = min(tm, m // dil)
    tn = min(tn, n_out)
    off = n_off // tn
    bps = m // dil // tm
    return pl.pallas_call(
        _mm_kernel,
        out_shape=jax.ShapeDtypeStruct((m, n_out), out_dtype),
        grid=(n_out // tn, m // tm),
        in_specs=[
            pl.BlockSpec((tm, k), lambda j, i: (i % bps, i // bps)),
            pl.BlockSpec((k, tn), lambda j, i: (0, j + off)),
        ],
        out_specs=pl.BlockSpec((tm, tn), lambda j, i: (i, j)),
        compiler_params=_params(("parallel", "parallel")),
        name="matmul",
    )(a.reshape(m // dil, dil * k), b)


def _layer_norm(v, gain, bias):
    mu = jnp.mean(v, axis=-1, keepdims=True)
    vc = v - mu
    var = jnp.mean(vc * vc, axis=-1, keepdims=True)
    return vc * lax.rsqrt(var + LN_EPS) * gain + bias


_V_GATE, _V_GAIN, _V_BIAS, _V_SCALE_A, _V_SHIFT_A, _V_SCALE_B, _V_SHIFT_B = range(7)


def _prologue_kernel(x_ref, vec_ref, o_ref):
    h = x_ref[...] * (1.0 + vec_ref[_V_SCALE_A:_V_SCALE_A + 1, :]) \
        + vec_ref[_V_SHIFT_A:_V_SHIFT_A + 1, :]
    o_ref[...] = h.astype(BF16)


def _prologue(x2, vecs, tm=256):
    s, d = x2.shape
    return pl.pallas_call(
        _prologue_kernel,
        out_shape=jax.ShapeDtypeStruct((s, d), BF16),
        grid=(s // tm,),
        in_specs=[pl.BlockSpec((tm, d), lambda i: (i, 0)),
                  pl.BlockSpec((8, d), lambda i: (0, 0))],
        out_specs=pl.BlockSpec((tm, d), lambda i: (i, 0)),
        compiler_params=_params(("parallel",)),
        name="prologue",
    )(x2, vecs)


def _ln_router_kernel(alpha, n_groups, n_per_group,
                      x_ref, mix_ref, vec_ref, wr_ref, br_ref, xo_ref, route_ref):
    gate = vec_ref[_V_GATE:_V_GATE + 1, :]
    x1 = alpha * x_ref[...] + (1.0 + gate) * mix_ref[...]
    xn = _layer_norm(x1, vec_ref[_V_GAIN:_V_GAIN + 1, :], vec_ref[_V_BIAS:_V_BIAS + 1, :])
    xo_ref[...] = xn
    hf = xn * (1.0 + vec_ref[_V_SCALE_A:_V_SCALE_A + 1, :]) + vec_ref[_V_SHIFT_A:_V_SHIFT_A + 1, :]
    logits = jnp.dot(hf, wr_ref[...], preferred_element_type=F32, precision=HIGHEST) + br_ref[...]
    lane = lax.broadcasted_iota(jnp.int32, logits.shape, 1).astype(F32)
    ninf = -jnp.inf
    big = float(LANES)
    is_g = lane < n_groups
    gl = jnp.where(is_g, logits, ninf)
    gmax = jnp.max(gl, axis=-1, keepdims=True)
    gsel = jnp.min(jnp.where(gl == gmax, lane, big), axis=-1, keepdims=True)
    p_group = 1.0 / jnp.sum(jnp.where(is_g, jnp.exp(logits - gmax), 0.0), axis=-1, keepdims=True)
    lo = n_groups + gsel * n_per_group
    el = jnp.where((lane >= lo) & (lane < lo + n_per_group), logits, ninf)
    m1 = jnp.max(el, axis=-1, keepdims=True)
    i1 = jnp.min(jnp.where(el == m1, lane, big), axis=-1, keepdims=True)
    el2 = jnp.where(lane == i1, ninf, el)
    m2 = jnp.max(el2, axis=-1, keepdims=True)
    i2 = jnp.min(jnp.where(el2 == m2, lane, big), axis=-1, keepdims=True)
    e21 = jnp.exp(m2 - m1)
    inv = 1.0 / (1.0 + e21)
    w1 = p_group * inv
    w2 = p_group * (e21 * inv)
    route_ref[...] = jnp.where(lane == 0.0, i1 - n_groups,
                     jnp.where(lane == 1.0, i2 - n_groups,
                     jnp.where(lane == 2.0, w1,
                     jnp.where(lane == 3.0, w2, 0.0))))


def _ln_router(x2, mix, vecs, wr, br, alpha, n_groups, n_per_group, tm=256):
    s, d = x2.shape
    return pl.pallas_call(
        functools.partial(_ln_router_kernel, alpha, n_groups, n_per_group),
        out_shape=[jax.ShapeDtypeStruct((s, d), F32),
                   jax.ShapeDtypeStruct((s, LANES), F32)],
        grid=(s // tm,),
        in_specs=[pl.BlockSpec((tm, d), lambda i: (i, 0)),
                  pl.BlockSpec((tm, d), lambda i: (i, 0)),
                  pl.BlockSpec((8, d), lambda i: (0, 0)),
                  pl.BlockSpec((d, LANES), lambda i: (0, 0)),
                  pl.BlockSpec((1, LANES), lambda i: (0, 0))],
        out_specs=[pl.BlockSpec((tm, d), lambda i: (i, 0)),
                   pl.BlockSpec((tm, LANES), lambda i: (i, 0))],
        compiler_params=_params(("parallel",)),
        name="ln_router",
    )(x2, mix, vecs, wr, br)


def _ln_combine_kernel(alpha, with_next, with_kv, x_ref, y0_ref, y1_ref, route_ref, vec_ref, *refs):
    ffn = y0_ref[...] * route_ref[:, 2:3] + y1_ref[...] * route_ref[:, 3:4]
    gate = vec_ref[_V_GATE:_V_GATE + 1, :]
    x1 = alpha * x_ref[...] + (1.0 + gate) * ffn
    xn = _layer_norm(x1, vec_ref[_V_GAIN:_V_GAIN + 1, :], vec_ref[_V_BIAS:_V_BIAS + 1, :])
    refs[0][...] = xn
    k = 1
    if with_kv:
        hk = xn * (1.0 + vec_ref[_V_SCALE_B:_V_SCALE_B + 1, :]) + vec_ref[_V_SHIFT_B:_V_SHIFT_B + 1, :]
        refs[k][...] = hk.astype(BF16)
        k += 1
    if with_next:
        h = xn * (1.0 + vec_ref[_V_SCALE_A:_V_SCALE_A + 1, :]) + vec_ref[_V_SHIFT_A:_V_SHIFT_A + 1, :]
        refs[k][...] = h.astype(BF16)


def _ln_combine(x2, y2, route, vecs, alpha, with_next, with_kv, tm=256):
    s, d = x2.shape
    row = lambda i: (i, 0)
    n_bf = int(with_kv) + int(with_next)
    shapes = [jax.ShapeDtypeStruct((s, d), F32)] + [jax.ShapeDtypeStruct((s, d), BF16)] * n_bf
    specs = [pl.BlockSpec((tm, d), row)] * (1 + n_bf)
    return pl.pallas_call(
        functools.partial(_ln_combine_kernel, alpha, with_next, with_kv),
        out_shape=shapes,
        grid=(s // tm,),
        in_specs=[pl.BlockSpec((tm, d), row),
                  pl.BlockSpec((tm, d), row),
                  pl.BlockSpec((tm, d), lambda i: (i + s // tm, 0)),
                  pl.BlockSpec((tm, LANES), row),
                  pl.BlockSpec((8, d), lambda i: (0, 0))],
        out_specs=specs,
        compiler_params=_params(("parallel",)),
        name="ln_combine",
    )(x2, y2, y2, route, vecs)


def _dilated_kernel(dil, blocks_per_stream, slopes, q_ref, kp_ref, kc_ref, vp_ref, vc_ref,
                    o_ref, lse_ref):
    nb = pl.program_id(0) % blocks_per_stream
    has_prev = nb > 0
    qi = lax.broadcasted_iota(jnp.int32, (A_BLOCK, A_BLOCK), 0)
    kj = lax.broadcasted_iota(jnp.int32, (A_BLOCK, A_BLOCK), 1)
    ok_prev = (kj >= qi) & has_prev
    ok_cur = kj <= qi
    dist_prev = ((qi + A_BLOCK - kj) * dil).astype(F32)
    dist_cur = ((qi - kj) * dil).astype(F32)
    inv_sqrt = 1.0 / math.sqrt(HEAD_DIM)
    lane = lax.broadcasted_iota(jnp.int32, (A_BLOCK, LANES), 1)
    lse_all = jnp.zeros((A_BLOCK, LANES), F32)
    for h, slope in enumerate(slopes):
        hs = slice(h * HEAD_DIM, (h + 1) * HEAD_DIM)
        q = q_ref[:, hs]
        s_p = lax.dot_general(q, kp_ref[:, hs], _NT, preferred_element_type=F32) * inv_sqrt
        s_c = lax.dot_general(q, kc_ref[:, hs], _NT, preferred_element_type=F32) * inv_sqrt
        s_p = jnp.where(ok_prev, s_p - slope * dist_prev, NEG)
        s_c = jnp.where(ok_cur, s_c - slope * dist_cur, NEG)
        m = jnp.maximum(jnp.max(s_p, axis=-1, keepdims=True), jnp.max(s_c, axis=-1, keepdims=True))
        p_p = jnp.exp(s_p - m)
        p_c = jnp.exp(s_c - m)
        l = jnp.sum(p_p, axis=-1, keepdims=True) + jnp.sum(p_c, axis=-1, keepdims=True)
        o = jnp.dot(p_p.astype(BF16), vp_ref[:, hs], preferred_element_type=F32) \
            + jnp.dot(p_c.astype(BF16), vc_ref[:, hs], preferred_element_type=F32)
        o_ref[:, hs] = o * (1.0 / l)
        lse_all = jnp.where(lane == h, m + jnp.log(l), lse_all)
    lse_ref[...] = lse_all


def _dilated_attention(qkv, dil, n_heads):
    s = qkv.shape[0]
    w = n_heads * HEAD_DIM
    n = s // dil
    bps = n // A_BLOCK
    slopes = [float(v) for v in np.exp2(-8.0 * np.arange(1, n_heads + 1, dtype=np.float32) / n_heads)]
    prev = lambda b: (jnp.maximum(b - 1, 0), 1)
    o, lse = pl.pallas_call(
        functools.partial(_dilated_kernel, dil, bps, slopes),
        out_shape=[jax.ShapeDtypeStruct((n, dil * w), F32),
                   jax.ShapeDtypeStruct((n, dil * LANES), F32)],
        grid=(s // A_BLOCK,),
        in_specs=[pl.BlockSpec((A_BLOCK, w), lambda b: (b, 0)),
                  pl.BlockSpec((A_BLOCK, w), prev),
                  pl.BlockSpec((A_BLOCK, w), lambda b: (b, 1)),
                  pl.BlockSpec((A_BLOCK, w), lambda b: (jnp.maximum(b - 1, 0), 2)),
                  pl.BlockSpec((A_BLOCK, w), lambda b: (b, 2))],
        out_specs=[pl.BlockSpec((A_BLOCK, w), lambda b: (b % bps, b // bps)),
                   pl.BlockSpec((A_BLOCK, LANES), lambda b: (b % bps, b // bps))],
        compiler_params=_params(("parallel",)),
        name="dilated_attention",
    )(qkv, qkv, qkv, qkv, qkv)
    return o.reshape(s, w), lse.reshape(s, LANES)


def _merge_kernel(n_heads, o0_ref, o1_ref, o2_ref, l0_ref, l1_ref, l2_ref, out_ref):
    l0, l1, l2 = l0_ref[...], l1_ref[...], l2_ref[...]
    m = jnp.maximum(jnp.maximum(l0, l1), l2)
    e0, e1, e2 = jnp.exp(l0 - m), jnp.exp(l1 - m), jnp.exp(l2 - m)
    inv = 1.0 / (e0 + e1 + e2)
    w0, w1, w2 = e0 * inv, e1 * inv, e2 * inv
    for h in range(n_heads):
        hs = slice(h * HEAD_DIM, (h + 1) * HEAD_DIM)
        acc = o0_ref[:, hs] * w0[:, h:h + 1] + o1_ref[:, hs] * w1[:, h:h + 1] \
            + o2_ref[:, hs] * w2[:, h:h + 1]
        out_ref[:, hs] = acc.astype(BF16)


def _merge_groups(os_, lses, n_heads, tm=256):
    s, w = os_[0].shape
    row = lambda i: (i, 0)
    return pl.pallas_call(
        functools.partial(_merge_kernel, n_heads),
        out_shape=jax.ShapeDtypeStruct((s, w), BF16),
        grid=(s // tm,),
        in_specs=[pl.BlockSpec((tm, w), row)] * 3 + [pl.BlockSpec((tm, LANES), row)] * 3,
        out_specs=pl.BlockSpec((tm, w), row),
        compiler_params=_params(("parallel",)),
        name="merge_groups",
    )(*os_, *lses)


def _kmean_kernel(k_ref, o_ref):
    o_ref[...] = jnp.mean(k_ref[...], axis=0, keepdims=True)


def _block_means(k):
    s, kw = k.shape
    nb = s // MOBA_BLOCK
    return pl.pallas_call(
        _kmean_kernel,
        out_shape=jax.ShapeDtypeStruct((nb, 1, kw), F32),
        grid=(nb,),
        in_specs=[pl.BlockSpec((MOBA_BLOCK, kw), lambda i: (i, 0))],
        out_specs=pl.BlockSpec((None, 1, kw), lambda i: (i, 0, 0)),
        compiler_params=_params(("parallel",)),
        name="block_means",
    )(k)


def _moba_kernel(q_ref, k_ref, vt_ref, kmean_ref, slope_ref, o_ref,
                 qs_ref, sel_ref, bias_ref, m_ref, l_ref, acc_ref):
    qb = pl.program_id(1)
    n_blk = kmean_ref.shape[0]
    rows = MOBA_REP * MOBA_BLOCK
    n_chunks = rows // LANES
    slope2 = slope_ref[...] * LOG2E

    @pl.when(qb == 0)
    def _():
        lane_t = (lax.broadcasted_iota(jnp.int32, (1, rows), 1) % MOBA_BLOCK).astype(F32)
        sub = lax.broadcasted_iota(jnp.int32, (MOBA_BLOCK, 1), 0).astype(F32)
        bias_ref[...] = slope2 * (lane_t - sub)

    q = q_ref[...]
    q4 = jnp.concatenate([q[:, r * HEAD_DIM:(r + 1) * HEAD_DIM] for r in range(MOBA_REP)], axis=0)
    gate = lax.dot_general(kmean_ref[...], q4, _NT, preferred_element_type=F32, precision=HIGHEST)
    blk = lax.broadcasted_iota(jnp.int32, (n_blk, rows), 0).astype(F32)
    ninf = -jnp.inf
    gate = jnp.where(blk < qb.astype(F32), gate, ninf)
    sel = jnp.zeros((n_blk, rows), F32)
    for _ in range(MOBA_TOPK):
        top = jnp.max(gate, axis=0, keepdims=True)
        idx = jnp.min(jnp.where(gate == top, blk, float(n_blk)), axis=0, keepdims=True)
        hit = (blk == idx) & (top > ninf)
        sel = jnp.where(hit, 1.0, sel)
        gate = jnp.where(hit, ninf, gate)
    sel_ref[...] = sel
    qs_ref[...] = (q4 * (LOG2E / math.sqrt(HEAD_DIM))).astype(BF16)

    kk = lax.broadcasted_iota(jnp.int32, (MOBA_BLOCK, LANES), 0)
    tt = lax.broadcasted_iota(jnp.int32, (MOBA_BLOCK, LANES), 1)

    def run_blocks(blocks, own):
        k_b = [k_ref[n] for n in blocks]
        vt_b = [vt_ref[n] for n in blocks]
        sel_b = [None if own else sel_ref[pl.ds(n, 1), :] for n in blocks]
        bt_b = [None if own else slope2 * ((n - qb) * MOBA_BLOCK).astype(F32) for n in blocks]
        tasks = [(bi, c) for bi in range(len(blocks)) for c in range(n_chunks)]

        def score(t):
            bi, c = tasks[t]
            return lax.dot_general(k_b[bi], qs_ref[c * LANES:(c + 1) * LANES, :], _NT,
                                   preferred_element_type=F32)

        ahead = 2
        pending = {t: score(t) for t in range(min(ahead, len(tasks)))}
        for t, (bi, c) in enumerate(tasks):
            cs = slice(c * LANES, (c + 1) * LANES)
            u = pending.pop(t) - bias_ref[:, cs]
            if own:
                u = jnp.where(kk <= tt + (c * LANES) % MOBA_BLOCK, u, NEG)
                m_new = jnp.max(u, axis=0, keepdims=True)
                p = jnp.exp2(u - m_new)
                l_new = jnp.sum(p, axis=0, keepdims=True)
            else:
                picked = sel_b[bi][:, cs] > 0.0
                bt = bt_b[bi][:, cs]
                m_old = m_ref[:, cs]
                m_new = jnp.maximum(
                    m_old, jnp.where(picked, jnp.max(u, axis=0, keepdims=True) + bt, NEG))
                p = jnp.exp2(u - jnp.where(picked, m_new - bt, -NEG))
                alpha = jnp.exp2(m_old - m_new)
                l_new = alpha * l_ref[:, cs] + jnp.sum(p, axis=0, keepdims=True)
            m_ref[:, cs] = m_new
            l_ref[:, cs] = l_new
            if t + ahead < len(tasks):
                pending[t + ahead] = score(t + ahead)
            pv = jnp.dot(vt_b[bi], p.astype(BF16), preferred_element_type=F32)
            acc_ref[:, cs] = pv if own else alpha * acc_ref[:, cs] + pv

    run_blocks([qb], own=True)

    def body(i, carry):
        run_blocks([i * MOBA_UNROLL + j for j in range(MOBA_UNROLL)], own=False)
        return carry

    lax.fori_loop(0, (qb + MOBA_UNROLL - 1) // MOBA_UNROLL, body, 0)
    o_t = acc_ref[...] * (1.0 / l_ref[...])
    for r in range(MOBA_REP):
        o_ref[:, r * HEAD_DIM:(r + 1) * HEAD_DIM] = \
            o_t[:, r * MOBA_BLOCK:(r + 1) * MOBA_BLOCK].T.astype(BF16)


def _moba_attention(q, k_blocks, vt_blocks, k_mean, slope_lanes):
    s = q.shape[0]
    g, nb = k_blocks.shape[:2]
    assert nb % MOBA_UNROLL == 0
    rows = MOBA_REP * MOBA_BLOCK
    qw = MOBA_REP * HEAD_DIM
    return pl.pallas_call(
        _moba_kernel,
        out_shape=jax.ShapeDtypeStruct((s, g * qw), BF16),
        grid=(g, nb),
        in_specs=[pl.BlockSpec((MOBA_BLOCK, qw), lambda gi, qb: (qb, gi)),
                  pl.BlockSpec((None, nb, MOBA_BLOCK, HEAD_DIM), lambda gi, qb: (gi, 0, 0, 0)),
                  pl.BlockSpec((None, nb, HEAD_DIM, MOBA_BLOCK), lambda gi, qb: (gi, 0, 0, 0)),
                  pl.BlockSpec((None, nb, HEAD_DIM), lambda gi, qb: (gi, 0, 0)),
                  pl.BlockSpec((None, 1, rows), lambda gi, qb: (gi, 0, 0))],
        out_specs=pl.BlockSpec((MOBA_BLOCK, qw), lambda gi, qb: (qb, gi)),
        scratch_shapes=[pltpu.VMEM((rows, HEAD_DIM), BF16),
                        pltpu.VMEM((nb, rows), F32),
                        pltpu.VMEM((MOBA_BLOCK, rows), F32),
                        pltpu.VMEM((1, rows), F32),
                        pltpu.VMEM((1, rows), F32),
                        pltpu.VMEM((HEAD_DIM, rows), F32)],
        compiler_params=_params(("parallel", "arbitrary")),
        name="moba_attention",
    )(q, k_blocks, vt_blocks, k_mean, slope_lanes)


def _moe_kernel(blk_e_ref, total_ref, blk_first_ref, slot_tok_ref, slot_dst_ref,
                x_hbm, vec_ref, wg_ref, wu_ref, wd_ref, y_hbm,
                xbuf, obuf, wg_bf, wu_bf, wd_bf, gsem, ssem):
    b = pl.program_id(0)
    total = total_ref[0]
    slot = b % 2

    def gather_copy(blk, j, buf_slot):
        tok = slot_tok_ref[blk * MOE_BLOCK + j]
        return pltpu.make_async_copy(x_hbm.at[pl.ds(tok, 1)], xbuf.at[buf_slot, pl.ds(j, 1)],
                                     gsem.at[buf_slot])

    def scatter_copy(blk, j, buf_slot):
        dst = slot_dst_ref[blk * MOE_BLOCK + j]
        return pltpu.make_async_copy(obuf.at[buf_slot, pl.ds(j, 1)], y_hbm.at[pl.ds(dst, 1)],
                                     ssem.at[buf_slot])

    def wait_gather(buf_slot):
        pltpu.make_async_copy(x_hbm.at[pl.ds(0, MOE_BLOCK)], xbuf.at[buf_slot],
                              gsem.at[buf_slot]).wait()

    def wait_scatter(buf_slot):
        pltpu.make_async_copy(obuf.at[buf_slot], y_hbm.at[pl.ds(0, MOE_BLOCK)],
                              ssem.at[buf_slot]).wait()

    @pl.when(b == 0)
    def _():
        n_real = y_hbm.shape[0] - 2 * MOE_BLOCK
        obuf[0] = jnp.zeros(obuf.shape[1:], F32)
        for i in range(2):
            dump = y_hbm.at[pl.ds(n_real + i * MOE_BLOCK, MOE_BLOCK)]
            pltpu.make_async_copy(obuf.at[0], dump, ssem.at[i]).start()
        for i in range(2):
            wait_scatter(i)
        for j in range(MOE_BLOCK):
            gather_copy(0, j, 0).start()

    @pl.when(b < total)
    def _():
        @pl.when(blk_first_ref[b] == 1)
        def _():
            wg_bf[...] = wg_ref[...].astype(BF16)
            wu_bf[...] = wu_ref[...].astype(BF16)
            wd_bf[...] = wd_ref[...].astype(BF16)

        wait_gather(slot)
        for j in range(MOE_BLOCK):
            gather_copy(b + 1, j, 1 - slot).start()
        hf = xbuf[slot] * (1.0 + vec_ref[_V_SCALE_A:_V_SCALE_A + 1, :]) \
            + vec_ref[_V_SHIFT_A:_V_SHIFT_A + 1, :]
        hb = hf.astype(BF16)
        gt = jnp.dot(hb, wg_bf[...], preferred_element_type=F32)
        up = jnp.dot(hb, wu_bf[...], preferred_element_type=F32)
        hid = gt * (1.0 / (1.0 + jnp.exp(-gt))) * up
        obuf[slot] = jnp.dot(hid.astype(BF16), wd_bf[...], preferred_element_type=F32)
        for j in range(MOE_BLOCK):
            scatter_copy(b, j, slot).start()

        @pl.when(b >= 1)
        def _():
            wait_scatter(1 - slot)

        @pl.when(b == total - 1)
        def _():
            wait_scatter(slot)
            wait_gather(1 - slot)


def _moe_dispatch(ids, n_experts):
    s, k = ids.shape
    a = s * k
    n_blocks = -(-a // MOE_BLOCK) + n_experts
    e_flat = ids.reshape(a)
    order = jnp.argsort(e_flat, stable=True).astype(jnp.int32)
    experts = jnp.arange(n_experts, dtype=jnp.int32)
    counts = jnp.sum((e_flat[:, None] == experts[None, :]).astype(jnp.int32), axis=0)
    starts = jnp.cumsum(counts) - counts
    nblk = (counts + MOE_BLOCK - 1) // MOE_BLOCK
    blk_end = jnp.cumsum(nblk)
    blk_start = blk_end - nblk
    total = blk_end[-1]
    bidx = jnp.arange(n_blocks, dtype=jnp.int32)
    used = bidx < total
    b_eff = jnp.minimum(bidx, total - 1)
    blk_e = jnp.sum((blk_end[None, :] <= b_eff[:, None]).astype(jnp.int32), axis=1)
    blk_e = jnp.minimum(blk_e, n_experts - 1)
    rank0 = (bidx - blk_start[blk_e]) * MOE_BLOCK
    blk_first = (used & (rank0 == 0)).astype(jnp.int32)
    lane = jnp.arange(MOE_BLOCK, dtype=jnp.int32)[None, :]
    rank = rank0[:, None] + lane
    valid = used[:, None] & (rank < counts[blk_e][:, None])
    src = jnp.clip(starts[blk_e][:, None] + rank, 0, a - 1)
    asg = order[src]
    slot_tok = jnp.where(valid, asg // k, 0)
    slot_tok = jnp.concatenate([slot_tok, jnp.zeros((1, MOE_BLOCK), jnp.int32)]).reshape(-1)
    dump = a + (bidx[:, None] % 2) * MOE_BLOCK + lane
    slot_dst = jnp.where(valid, (asg % k) * s + asg // k, dump).reshape(-1)
    return blk_e, total.reshape(1), blk_first, slot_tok, slot_dst, n_blocks


def _moe_experts(x2, vecs, ids, layer, w_gate, w_up, w_down):
    s, d = x2.shape
    n_experts, _, f = w_gate.shape[1:]
    k = ids.shape[1]
    blk_e, total, blk_first, slot_tok, slot_dst, n_blocks = _moe_dispatch(ids, n_experts)
    w_idx = lambda b, be, *_: (layer, be[b], 0, 0)
    return pl.pallas_call(
        _moe_kernel,
        out_shape=jax.ShapeDtypeStruct((s * k + 2 * MOE_BLOCK, d), F32),
        grid_spec=pltpu.PrefetchScalarGridSpec(
            num_scalar_prefetch=5,
            grid=(n_blocks,),
            in_specs=[pl.BlockSpec(memory_space=pl.ANY),
                      pl.BlockSpec((8, d), lambda b, *_: (0, 0)),
                      pl.BlockSpec((None, None, d, f), w_idx),
                      pl.BlockSpec((None, None, d, f), w_idx),
                      pl.BlockSpec((None, None, f, d), w_idx)],
            out_specs=pl.BlockSpec(memory_space=pl.ANY),
            scratch_shapes=[pltpu.VMEM((2, MOE_BLOCK, d), F32),
                            pltpu.VMEM((2, MOE_BLOCK, d), F32),
                            pltpu.VMEM((d, f), BF16),
                            pltpu.VMEM((d, f), BF16),
                            pltpu.VMEM((f, d), BF16),
                            pltpu.SemaphoreType.DMA((2,)),
                            pltpu.SemaphoreType.DMA((2,))]),
        compiler_params=_params(("arbitrary",)),
        name="moe_experts",
    )(blk_e, total, blk_first, slot_tok, slot_dst, x2, vecs, w_gate, w_up, w_down)


def _vec_pack(d, **rows):
    out = jnp.zeros((8, d), F32)
    for name, v in rows.items():
        out = out.at[name_to_row[name]].set(v)
    return out


name_to_row = {"gate": _V_GATE, "gain": _V_GAIN, "bias": _V_BIAS, "scale_a": _V_SCALE_A,
               "shift_a": _V_SHIFT_A, "scale_b": _V_SCALE_B, "shift_b": _V_SHIFT_B}


def kernel(x, c, ada_down, ada_up, ada_bias, ln_gain, ln_bias, a_w_qkv, a_w_o, kv_ada_down, kv_ada_up, kv_ada_bias, kv_w, b_w_q, b_w_o, moe_w_group, moe_b_group, moe_w_expert, moe_b_expert, moe_w_gate, moe_w_up, moe_w_down):
    batch, s, d = x.shape
    assert batch == 1
    depth = ada_down.shape[0]
    n_a = a_w_qkv.shape[0]
    alpha = (2 * depth) ** 0.25
    a_heads = a_w_o.shape[1] // HEAD_DIM
    a_width = a_heads * HEAD_DIM
    n_groups = moe_w_group.shape[-1]
    n_per_group = moe_w_expert.shape[-1]
    kv_heads = kv_w.shape[1] // (2 * HEAD_DIM)
    kv_width = kv_heads * HEAD_DIM
    b_heads = kv_heads * MOBA_REP
    n_blk = s // MOBA_BLOCK

    mods = _modulation(c, ada_down, ada_up, ada_bias).reshape(depth, 6, d)
    kv_mod = _modulation(c, kv_ada_down[None], kv_ada_up[None], kv_ada_bias[None]).reshape(2, d)

    x2 = x.reshape(s, d)
    hs = _prologue(x2, _vec_pack(d, scale_a=mods[0, 1], shift_a=mods[0, 0]))

    k_blocks = vt_blocks = k_mean = slope_lanes = None
    for l in range(depth):
        shift1, scale1, gate1, shift2, scale2, gate2 = [mods[l, i] for i in range(6)]
        if l < n_a:
            w_qkv = a_w_qkv[l].astype(BF16)
            outs, lses = [], []
            for g, (_, dil) in enumerate(A_GROUPS):
                qkv = _matmul(hs, w_qkv, BF16, n_off=g * 3 * a_width, n_out=3 * a_width, dil=dil)
                o_g, lse_g = _dilated_attention(qkv, dil, a_heads)
                outs.append(o_g)
                lses.append(lse_g)
            merged = _merge_groups(outs, lses, a_heads)
            mix = _matmul(merged, a_w_o[l].astype(BF16), F32)
        else:
            j = l - n_a
            q = _matmul(hs, b_w_q[j].astype(BF16), F32)
            o = _moba_attention(q, k_blocks, vt_blocks, k_mean, slope_lanes)
            mix = _matmul(o, b_w_o[j].astype(BF16), F32)

        wr = jnp.concatenate(
            [moe_w_group[l], jnp.moveaxis(moe_w_expert[l], 0, 1).reshape(d, n_groups * n_per_group),
             jnp.zeros((d, LANES - n_groups * (1 + n_per_group)), F32)], axis=1)
        br = jnp.concatenate(
            [moe_b_group[l], moe_b_expert[l].reshape(-1),
             jnp.zeros((LANES - n_groups * (1 + n_per_group),), F32)]).reshape(1, LANES)
        vec1 = _vec_pack(d, gate=gate1, gain=ln_gain[l, 0], bias=ln_bias[l, 0],
                         scale_a=scale2, shift_a=shift2)
        x2, route = _ln_router(x2, mix, vec1, wr, br, alpha, n_groups, n_per_group)
        ids = route[:, :MOE_TOPK].astype(jnp.int32)
        y2 = _moe_experts(x2, vec1, ids, l, moe_w_gate, moe_w_up, moe_w_down)

        last = l == depth - 1
        with_kv = l == n_a - 1 and n_a < depth
        rows = dict(gate=gate2, gain=ln_gain[l, 1], bias=ln_bias[l, 1])
        if not last:
            rows.update(scale_a=mods[l + 1, 1], shift_a=mods[l + 1, 0])
        if with_kv:
            rows.update(scale_b=kv_mod[1], shift_b=kv_mod[0])
        res = _ln_combine(x2, y2, route, _vec_pack(d, **rows), alpha, not last, with_kv)
        x2 = res[0]
        if with_kv:
            kv = _matmul(res[1], kv_w.astype(BF16), F32)
            k_f32 = kv[:, :kv_width]
            k_mean = _block_means(k_f32).reshape(n_blk, kv_heads, HEAD_DIM).transpose(1, 0, 2)
            k_blocks = k_f32.astype(BF16).reshape(n_blk, MOBA_BLOCK, kv_heads, HEAD_DIM) \
                .transpose(2, 0, 1, 3)
            vt_blocks = kv[:, kv_width:].astype(BF16).reshape(n_blk, MOBA_BLOCK, kv_heads, HEAD_DIM) \
                .transpose(2, 0, 3, 1)
            slopes = jnp.exp2(-8.0 * jnp.arange(1, b_heads + 1, dtype=F32) / b_heads)
            slope_lanes = jnp.repeat(slopes.reshape(kv_heads, MOBA_REP), MOBA_BLOCK, axis=1) \
                .reshape(kv_heads, 1, MOBA_REP * MOBA_BLOCK)
        if not last:
            hs = res[-1]
    return x2.reshape(batch, s, d)
```

```python
import functools
import math

import numpy as np
import jax
import jax.numpy as jnp
from jax import lax
from jax.experimental import pallas as pl
from jax.experimental.pallas import tpu as pltpu

F32 = jnp.float32
BF16 = jnp.bfloat16
HIGHEST = lax.Precision.HIGHEST

HEAD_DIM = 128
A_GROUPS = ((128, 1), (512, 4), (2048, 16))
A_BLOCK = 128
MOBA_BLOCK = 256
MOBA_TOPK = 3
MOBA_REP = 4
MOBA_UNROLL = 4
MOE_TOPK = 2
MOE_BLOCK = 128
LN_EPS = 1e-5
NEG = -1e30
LOG2E = math.log2(math.e)
LANES = 128
VMEM_LIMIT = 56 * 1024 * 1024

_NT = (((1,), (1,)), ((), ()))


def _params(sem, vmem=VMEM_LIMIT):
    return pltpu.CompilerParams(dimension_semantics=sem, vmem_limit_bytes=vmem)


def _mod_kernel(c_ref, wd_ref, wu_ref, b_ref, o_ref, t_ref):
    @pl.when(pl.program_id(1) == 0)
    def _():
        c = c_ref[...]
        sc = c * (1.0 / (1.0 + jnp.exp(-c)))
        t_ref[...] = jnp.dot(sc, wd_ref[...], preferred_element_type=F32, precision=HIGHEST)

    o_ref[...] = jnp.dot(t_ref[...], wu_ref[...], preferred_element_type=F32,
                         precision=HIGHEST) + b_ref[...]


def _modulation(c, w_down, w_up, b_up):
    n_l, d, r = w_down.shape
    n = w_up.shape[-1]
    tn = n // pl.cdiv(n, 6144)
    c8 = jnp.broadcast_to(c, (8, d))
    out = pl.pallas_call(
        _mod_kernel,
        out_shape=jax.ShapeDtypeStruct((n_l, 8, n), F32),
        grid=(n_l, n // tn),
        in_specs=[
            pl.BlockSpec((8, d), lambda l, j: (0, 0)),
            pl.BlockSpec((None, d, r), lambda l, j: (l, 0, 0)),
            pl.BlockSpec((None, r, tn), lambda l, j: (l, 0, j)),
            pl.BlockSpec((None, 1, tn), lambda l, j: (l, 0, j)),
        ],
        out_specs=pl.BlockSpec((None, 8, tn), lambda l, j: (l, 0, j)),
        scratch_shapes=[pltpu.VMEM((8, r), F32)],
        compiler_params=_params(("arbitrary", "arbitrary")),
        name="modulation",
    )(c8, w_down, w_up, b_up.reshape(n_l, 1, n))
    return out[:, 0, :]


def _mm_kernel(a_ref, b_ref, o_ref):
    o_ref[...] = jnp.dot(a_ref[...], b_ref[...],
                         preferred_element_type=F32).astype(o_ref.dtype)


def _matmul(a, b, out_dtype, n_off=0, n_out=None, dil=1, tm=512, tn=1024):
    m, k = a.shape
    n_out = b.shape[1] if n_out is None else n_out
    tm = min(tm, m // dil)
    tn = min(tn, n_out)
    off = n_off // tn
    bps = m // dil // tm
    return pl.pallas_call(
        _mm_kernel,
        out_shape=jax.ShapeDtypeStruct((m, n_out), out_dtype),
        grid=(n_out // tn, m // tm),
        in_specs=[
            pl.BlockSpec((tm, k), lambda j, i: (i % bps, i // bps)),
            pl.BlockSpec((k, tn), lambda j, i: (0, j + off)),
        ],
        out_specs=pl.BlockSpec((tm, tn), lambda j, i: (i, j)),
        compiler_params=_params(("parallel", "parallel")),
        name="matmul",
    )(a.reshape(m // dil, dil * k), b)


def _layer_norm(v, gain, bias):
    mu = jnp.mean(v, axis=-1, keepdims=True)
    vc = v - mu
    var = jnp.mean(vc * vc, axis=-1, keepdims=True)
    return vc * lax.rsqrt(var + LN_EPS) * gain + bias


_V_GATE, _V_GAIN, _V_BIAS, _V_SCALE_A, _V_SHIFT_A, _V_SCALE_B, _V_SHIFT_B = range(7)


def _prologue_kernel(x_ref, vec_ref, o_ref):
    h = x_ref[...] * (1.0 + vec_ref[_V_SCALE_A:_V_SCALE_A + 1, :]) \
        + vec_ref[_V_SHIFT_A:_V_SHIFT_A + 1, :]
    o_ref[...] = h.astype(BF16)


def _prologue(x2, vecs, tm=256):
    s, d = x2.shape
    return pl.pallas_call(
        _prologue_kernel,
        out_shape=jax.ShapeDtypeStruct((s, d), BF16),
        grid=(s // tm,),
        in_specs=[pl.BlockSpec((tm, d), lambda i: (i, 0)),
                  pl.BlockSpec((8, d), lambda i: (0, 0))],
        out_specs=pl.BlockSpec((tm, d), lambda i: (i, 0)),
        compiler_params=_params(("parallel",)),
        name="prologue",
    )(x2, vecs)


def _ln_router_kernel(alpha, n_groups, n_per_group,
                      x_ref, mix_ref, vec_ref, wr_ref, br_ref, xo_ref, route_ref):
    gate = vec_ref[_V_GATE:_V_GATE + 1, :]
    x1 = alpha * x_ref[...] + (1.0 + gate) * mix_ref[...]
    xn = _layer_norm(x1, vec_ref[_V_GAIN:_V_GAIN + 1, :], vec_ref[_V_BIAS:_V_BIAS + 1, :])
    xo_ref[...] = xn
    hf = xn * (1.0 + vec_ref[_V_SCALE_A:_V_SCALE_A + 1, :]) + vec_ref[_V_SHIFT_A:_V_SHIFT_A + 1, :]
    logits = jnp.dot(hf, wr_ref[...], preferred_element_type=F32, precision=HIGHEST) + br_ref[...]
    lane = lax.broadcasted_iota(jnp.int32, logits.shape, 1).astype(F32)
    ninf = -jnp.inf
    big = float(LANES)
    is_g = lane < n_groups
    gl = jnp.where(is_g, logits, ninf)
    gmax = jnp.max(gl, axis=-1, keepdims=True)
    gsel = jnp.min(jnp.where(gl == gmax, lane, big), axis=-1, keepdims=True)
    p_group = 1.0 / jnp.sum(jnp.where(is_g, jnp.exp(logits - gmax), 0.0), axis=-1, keepdims=True)
    lo = n_groups + gsel * n_per_group
    el = jnp.where((lane >= lo) & (lane < lo + n_per_group), logits, ninf)
    m1 = jnp.max(el, axis=-1, keepdims=True)
    i1 = jnp.min(jnp.where(el == m1, lane, big), axis=-1, keepdims=True)
    el2 = jnp.where(lane == i1, ninf, el)
    m2 = jnp.max(el2, axis=-1, keepdims=True)
    i2 = jnp.min(jnp.where(el2 == m2, lane, big), axis=-1, keepdims=True)
    e21 = jnp.exp(m2 - m1)
    inv = 1.0 / (1.0 + e21)
    w1 = p_group * inv
    w2 = p_group * (e21 * inv)
    route_ref[...] = jnp.where(lane == 0.0, i1 - n_groups,
                     jnp.where(lane == 1.0, i2 - n_groups,
                     jnp.where(lane == 2.0, w1,
                     jnp.where(lane == 3.0, w2, 0.0))))


def _ln_router(x2, mix, vecs, wr, br, alpha, n_groups, n_per_group, tm=256):
    s, d = x2.shape
    return pl.pallas_call(
        functools.partial(_ln_router_kernel, alpha, n_groups, n_per_group),
        out_shape=[jax.ShapeDtypeStruct((s, d), F32),
                   jax.ShapeDtypeStruct((s, LANES), F32)],
        grid=(s // tm,),
        in_specs=[pl.BlockSpec((tm, d), lambda i: (i, 0)),
                  pl.BlockSpec((tm, d), lambda i: (i, 0)),
                  pl.BlockSpec((8, d), lambda i: (0, 0)),
                  pl.BlockSpec((d, LANES), lambda i: (0, 0)),
                  pl.BlockSpec((1, LANES), lambda i: (0, 0))],
        out_specs=[pl.BlockSpec((tm, d), lambda i: (i, 0)),
                   pl.BlockSpec((tm, LANES), lambda i: (i, 0))],
        compiler_params=_params(("parallel",)),
        name="ln_router",
    )(x2, mix, vecs, wr, br)


def _ln_combine_kernel(alpha, with_next, with_kv, x_ref, y0_ref, y1_ref, route_ref, vec_ref, *refs):
    ffn = y0_ref[...] * route_ref[:, 2:3] + y1_ref[...] * route_ref[:, 3:4]
    gate = vec_ref[_V_GATE:_V_GATE + 1, :]
    x1 = alpha * x_ref[...] + (1.0 + gate) * ffn
    xn = _layer_norm(x1, vec_ref[_V_GAIN:_V_GAIN + 1, :], vec_ref[_V_BIAS:_V_BIAS + 1, :])
    refs[0][...] = xn
    k = 1
    if with_kv:
        hk = xn * (1.0 + vec_ref[_V_SCALE_B:_V_SCALE_B + 1, :]) + vec_ref[_V_SHIFT_B:_V_SHIFT_B + 1, :]
        refs[k][...] = hk.astype(BF16)
        k += 1
    if with_next:
        h = xn * (1.0 + vec_ref[_V_SCALE_A:_V_SCALE_A + 1, :]) + vec_ref[_V_SHIFT_A:_V_SHIFT_A + 1, :]
        refs[k][...] = h.astype(BF16)


def _ln_combine(x2, y2, route, vecs, alpha, with_next, with_kv, tm=256):
    s, d = x2.shape
    row = lambda i: (i, 0)
    n_bf = int(with_kv) + int(with_next)
    shapes = [jax.ShapeDtypeStruct((s, d), F32)] + [jax.ShapeDtypeStruct((s, d), BF16)] * n_bf
    specs = [pl.BlockSpec((tm, d), row)] * (1 + n_bf)
    return pl.pallas_call(
        functools.partial(_ln_combine_kernel, alpha, with_next, with_kv),
        out_shape=shapes,
        grid=(s // tm,),
        in_specs=[pl.BlockSpec((tm, d), row),
                  pl.BlockSpec((tm, d), row),
                  pl.BlockSpec((tm, d), lambda i: (i + s // tm, 0)),
                  pl.BlockSpec((tm, LANES), row),
                  pl.BlockSpec((8, d), lambda i: (0, 0))],
        out_specs=specs,
        compiler_params=_params(("parallel",)),
        name="ln_combine",
    )(x2, y2, y2, route, vecs)


def _dilated_kernel(dil, blocks_per_stream, slopes, q_ref, kp_ref, kc_ref, vp_ref, vc_ref,
                    o_ref, lse_ref):
    nb = pl.program_id(0) % blocks_per_stream
    has_prev = nb > 0
    qi = lax.broadcasted_iota(jnp.int32, (A_BLOCK, A_BLOCK), 0)
    kj = lax.broadcasted_iota(jnp.int32, (A_BLOCK, A_BLOCK), 1)
    dist_prev = jnp.where((kj >= qi) & has_prev, ((qi + A_BLOCK - kj) * dil).astype(F32), -NEG)
    dist_cur = jnp.where(kj <= qi, ((qi - kj) * dil).astype(F32), -NEG)
    sqrt_e = math.sqrt(HEAD_DIM)
    lane = lax.broadcasted_iota(jnp.int32, (A_BLOCK, LANES), 1)
    ones = jnp.ones((A_BLOCK, LANES), BF16)
    n_heads = len(slopes)

    def scores(h):
        hs = slice(h * HEAD_DIM, (h + 1) * HEAD_DIM)
        q = q_ref[:, hs]
        return (lax.dot_general(q, kp_ref[:, hs], _NT, preferred_element_type=F32),
                lax.dot_general(q, kc_ref[:, hs], _NT, preferred_element_type=F32))

    ahead = 6
    pending = {h: scores(h) for h in range(min(ahead, n_heads))}
    lse_all = jnp.zeros((A_BLOCK, LANES), F32)
    for h, slope in enumerate(slopes):
        hs = slice(h * HEAD_DIM, (h + 1) * HEAD_DIM)
        s_p, s_c = pending.pop(h)
        t_p = s_p - (slope * sqrt_e) * dist_prev
        t_c = s_c - (slope * sqrt_e) * dist_cur
        t_max = jnp.max(jnp.maximum(t_p, t_c), axis=-1, keepdims=True)
        p_p = jnp.exp2((t_p - t_max) * (LOG2E / sqrt_e)).astype(BF16)
        p_c = jnp.exp2((t_c - t_max) * (LOG2E / sqrt_e)).astype(BF16)
        if h + ahead < n_heads:
            pending[h + ahead] = scores(h + ahead)
        o = jnp.dot(p_p, vp_ref[:, hs], preferred_element_type=F32) \
            + jnp.dot(p_c, vc_ref[:, hs], preferred_element_type=F32)
        l = jnp.dot(p_p, ones, preferred_element_type=F32) \
            + jnp.dot(p_c, ones, preferred_element_type=F32)
        o_ref[:, hs] = o * (1.0 / l)
        lse_all = jnp.where(lane == h, t_max * (1.0 / sqrt_e) + jnp.log(l), lse_all)
    lse_ref[...] = lse_all


def _dilated_attention(qkv, dil, n_heads):
    s = qkv.shape[0]
    w = n_heads * HEAD_DIM
    n = s // dil
    bps = n // A_BLOCK
    slopes = [float(v) for v in np.exp2(-8.0 * np.arange(1, n_heads + 1, dtype=np.float32) / n_heads)]
    prev = lambda b: (jnp.maximum(b - 1, 0), 1)
    o, lse = pl.pallas_call(
        functools.partial(_dilated_kernel, dil, bps, slopes),
        out_shape=[jax.ShapeDtypeStruct((n, dil * w), F32),
                   jax.ShapeDtypeStruct((n, dil * LANES), F32)],
        grid=(s // A_BLOCK,),
        in_specs=[pl.BlockSpec((A_BLOCK, w), lambda b: (b, 0)),
                  pl.BlockSpec((A_BLOCK, w), prev),
                  pl.BlockSpec((A_BLOCK, w), lambda b: (b, 1)),
                  pl.BlockSpec((A_BLOCK, w), lambda b: (jnp.maximum(b - 1, 0), 2)),
                  pl.BlockSpec((A_BLOCK, w), lambda b: (b, 2))],
        out_specs=[pl.BlockSpec((A_BLOCK, w), lambda b: (b % bps, b // bps)),
                   pl.BlockSpec((A_BLOCK, LANES), lambda b: (b % bps, b // bps))],
        compiler_params=_params(("parallel",)),
        name="dilated_attention",
    )(qkv, qkv, qkv, qkv, qkv)
    return o, lse


def _merge_kernel(n_heads, *refs):
    n_g = len(A_GROUPS)
    o_refs, l_refs = refs[:n_g], refs[n_g:2 * n_g]
    out_ref = refs[2 * n_g]
    stages = iter(refs[2 * n_g + 1:])
    tm = out_ref.shape[0]
    o_nat, l_nat = [], []
    for (_, dil), o_ref, l_ref in zip(A_GROUPS, o_refs, l_refs):
        if dil == 1:
            o_nat.append(lambda h, o_ref=o_ref: o_ref[:, h * HEAD_DIM:(h + 1) * HEAD_DIM])
            l_nat.append(l_ref[...])
            continue
        stage = next(stages)
        rows = tm // dil
        w = o_ref.shape[1] // dil
        for r in range(dil):
            for h in range(n_heads):
                stage[h, pl.ds(r, rows, stride=dil), :] = \
                    o_ref[:, r * w + h * HEAD_DIM:r * w + (h + 1) * HEAD_DIM]
            stage[n_heads, pl.ds(r, rows, stride=dil), :] = l_ref[:, r * LANES:(r + 1) * LANES]
        o_nat.append(lambda h, stage=stage: stage[h])
        l_nat.append(stage[n_heads])
    m = functools.reduce(jnp.maximum, l_nat)
    es = [jnp.exp(l - m) for l in l_nat]
    inv = 1.0 / functools.reduce(lambda a, b: a + b, es)
    ws = [e * inv for e in es]
    for h in range(n_heads):
        acc = functools.reduce(lambda a, b: a + b,
                               [o(h) * w_g[:, h:h + 1] for o, w_g in zip(o_nat, ws)])
        out_ref[:, h * HEAD_DIM:(h + 1) * HEAD_DIM] = acc.astype(BF16)


def _merge_groups(os_, lses, n_heads, tm=256):
    w = n_heads * HEAD_DIM
    s = os_[0].shape[0] * A_GROUPS[0][1]
    row = lambda i: (i, 0)
    dils = [dil for _, dil in A_GROUPS]
    return pl.pallas_call(
        functools.partial(_merge_kernel, n_heads),
        out_shape=jax.ShapeDtypeStruct((s, w), BF16),
        grid=(s // tm,),
        in_specs=[pl.BlockSpec((tm // dil, dil * w), row) for dil in dils]
        + [pl.BlockSpec((tm // dil, dil * LANES), row) for dil in dils],
        out_specs=pl.BlockSpec((tm, w), row),
        scratch_shapes=[pltpu.VMEM((n_heads + 1, tm, LANES), F32) for dil in dils if dil > 1],
        compiler_params=_params(("parallel",)),
        name="merge_groups",
    )(*os_, *lses)


def _kmean_kernel(k_ref, o_ref):
    o_ref[...] = jnp.mean(k_ref[...], axis=0, keepdims=True)


def _block_means(k):
    s, kw = k.shape
    nb = s // MOBA_BLOCK
    return pl.pallas_call(
        _kmean_kernel,
        out_shape=jax.ShapeDtypeStruct((nb, 1, kw), F32),
        grid=(nb,),
        in_specs=[pl.BlockSpec((MOBA_BLOCK, kw), lambda i: (i, 0))],
        out_specs=pl.BlockSpec((None, 1, kw), lambda i: (i, 0, 0)),
        compiler_params=_params(("parallel",)),
        name="block_means",
    )(k)


def _moba_kernel(q_ref, k_ref, vt_ref, kmean_ref, slope_ref, o_ref,
                 qs_ref, sel_ref, bias_ref, m_ref, l_ref, acc_ref):
    qb = pl.program_id(1)
    n_blk = kmean_ref.shape[0]
    rows = MOBA_REP * MOBA_BLOCK
    n_chunks = rows // LANES
    slope2 = slope_ref[...] * LOG2E

    @pl.when(qb == 0)
    def _():
        lane_t = (lax.broadcasted_iota(jnp.int32, (1, rows), 1) % MOBA_BLOCK).astype(F32)
        sub = lax.broadcasted_iota(jnp.int32, (MOBA_BLOCK, 1), 0).astype(F32)
        bias_ref[...] = slope2 * (lane_t - sub)

    q = q_ref[...]
    q4 = jnp.concatenate([q[:, r * HEAD_DIM:(r + 1) * HEAD_DIM] for r in range(MOBA_REP)], axis=0)
    gate = lax.dot_general(kmean_ref[...], q4, _NT, preferred_element_type=F32, precision=HIGHEST)
    blk = lax.broadcasted_iota(jnp.int32, (n_blk, rows), 0).astype(F32)
    ninf = -jnp.inf
    gate = jnp.where(blk < qb.astype(F32), gate, ninf)
    sel = jnp.zeros((n_blk, rows), F32)
    for _ in range(MOBA_TOPK):
        top = jnp.max(gate, axis=0, keepdims=True)
        idx = jnp.min(jnp.where(gate == top, blk, float(n_blk)), axis=0, keepdims=True)
        hit = (blk == idx) & (top > ninf)
        sel = jnp.where(hit, 1.0, sel)
        gate = jnp.where(hit, ninf, gate)
    sel_ref[...] = sel
    qs_ref[...] = (q4 * (LOG2E / math.sqrt(HEAD_DIM))).astype(BF16)

    kk = lax.broadcasted_iota(jnp.int32, (MOBA_BLOCK, LANES), 0)
    tt = lax.broadcasted_iota(jnp.int32, (MOBA_BLOCK, LANES), 1)

    def run_blocks(blocks, own):
        k_b = [k_ref[n] for n in blocks]
        vt_b = [vt_ref[n] for n in blocks]
        sel_b = [None if own else sel_ref[pl.ds(n, 1), :] for n in blocks]
        bt_b = [None if own else slope2 * ((n - qb) * MOBA_BLOCK).astype(F32) for n in blocks]
        tasks = [(bi, c) for bi in range(len(blocks)) for c in range(n_chunks)]

        def score(t):
            bi, c = tasks[t]
            return lax.dot_general(k_b[bi], qs_ref[c * LANES:(c + 1) * LANES, :], _NT,
                                   preferred_element_type=F32)

        ahead = 2
        pending = {t: score(t) for t in range(min(ahead, len(tasks)))}
        for t, (bi, c) in enumerate(tasks):
            cs = slice(c * LANES, (c + 1) * LANES)
            u = pending.pop(t) - bias_ref[:, cs]
            if own:
                u = jnp.where(kk <= tt + (c * LANES) % MOBA_BLOCK, u, NEG)
                m_new = jnp.max(u, axis=0, keepdims=True)
                p = jnp.exp2(u - m_new)
                l_new = jnp.sum(p, axis=0, keepdims=True)
            else:
                picked = sel_b[bi][:, cs] > 0.0
                bt = bt_b[bi][:, cs]
                m_old = m_ref[:, cs]
                m_new = jnp.maximum(
                    m_old, jnp.where(picked, jnp.max(u, axis=0, keepdims=True) + bt, NEG))
                p = jnp.exp2(u - jnp.where(picked, m_new - bt, -NEG))
                alpha = jnp.exp2(m_old - m_new)
                l_new = alpha * l_ref[:, cs] + jnp.sum(p, axis=0, keepdims=True)
            m_ref[:, cs] = m_new
            l_ref[:, cs] = l_new
            if t + ahead < len(tasks):
                pending[t + ahead] = score(t + ahead)
            pv = jnp.dot(vt_b[bi], p.astype(BF16), preferred_element_type=F32)
            acc_ref[:, cs] = pv if own else alpha * acc_ref[:, cs] + pv

    run_blocks([qb], own=True)

    def body(i, carry):
        run_blocks([i * MOBA_UNROLL + j for j in range(MOBA_UNROLL)], own=False)
        return carry

    lax.fori_loop(0, (qb + MOBA_UNROLL - 1) // MOBA_UNROLL, body, 0)
    o_t = acc_ref[...] * (1.0 / l_ref[...])
    for r in range(MOBA_REP):
        o_ref[:, r * HEAD_DIM:(r + 1) * HEAD_DIM] = \
            o_t[:, r * MOBA_BLOCK:(r + 1) * MOBA_BLOCK].T.astype(BF16)


def _moba_attention(q, k_blocks, vt_blocks, k_mean, slope_lanes):
    s = q.shape[0]
    g, nb = k_blocks.shape[:2]
    assert nb % MOBA_UNROLL == 0
    rows = MOBA_REP * MOBA_BLOCK
    qw = MOBA_REP * HEAD_DIM
    return pl.pallas_call(
        _moba_kernel,
        out_shape=jax.ShapeDtypeStruct((s, g * qw), BF16),
        grid=(g, nb),
        in_specs=[pl.BlockSpec((MOBA_BLOCK, qw), lambda gi, qb: (qb, gi)),
                  pl.BlockSpec((None, nb, MOBA_BLOCK, HEAD_DIM), lambda gi, qb: (gi, 0, 0, 0)),
                  pl.BlockSpec((None, nb, HEAD_DIM, MOBA_BLOCK), lambda gi, qb: (gi, 0, 0, 0)),
                  pl.BlockSpec((None, nb, HEAD_DIM), lambda gi, qb: (gi, 0, 0)),
                  pl.BlockSpec((None, 1, rows), lambda gi, qb: (gi, 0, 0))],
        out_specs=pl.BlockSpec((MOBA_BLOCK, qw), lambda gi, qb: (qb, gi)),
        scratch_shapes=[pltpu.VMEM((rows, HEAD_DIM), BF16),
                        pltpu.VMEM((nb, rows), F32),
                        pltpu.VMEM((MOBA_BLOCK, rows), F32),
                        pltpu.VMEM((1, rows), F32),
                        pltpu.VMEM((1, rows), F32),
                        pltpu.VMEM((HEAD_DIM, rows), F32)],
        compiler_params=_params(("parallel", "arbitrary")),
        name="moba_attention",
    )(q, k_blocks, vt_blocks, k_mean, slope_lanes)


def _moe_kernel(blk_e_ref, total_ref, blk_first_ref, slot_tok_ref, slot_dst_ref,
                x_hbm, vec_ref, wg_ref, wu_ref, wd_ref, y_hbm,
                xbuf, obuf, wg_bf, wu_bf, wd_bf, gsem, ssem):
    b = pl.program_id(0)
    total = total_ref[0]
    slot = b % 2

    def gather_copy(blk, j, buf_slot):
        tok = slot_tok_ref[blk * MOE_BLOCK + j]
        return pltpu.make_async_copy(x_hbm.at[pl.ds(tok, 1)], xbuf.at[buf_slot, pl.ds(j, 1)],
                                     gsem.at[buf_slot])

    def scatter_copy(blk, j, buf_slot):
        dst = slot_dst_ref[blk * MOE_BLOCK + j]
        return pltpu.make_async_copy(obuf.at[buf_slot, pl.ds(j, 1)], y_hbm.at[pl.ds(dst, 1)],
                                     ssem.at[buf_slot])

    def wait_gather(buf_slot):
        pltpu.make_async_copy(x_hbm.at[pl.ds(0, MOE_BLOCK)], xbuf.at[buf_slot],
                              gsem.at[buf_slot]).wait()

    def wait_scatter(buf_slot):
        pltpu.make_async_copy(obuf.at[buf_slot], y_hbm.at[pl.ds(0, MOE_BLOCK)],
                              ssem.at[buf_slot]).wait()

    @pl.when(b == 0)
    def _():
        n_real = y_hbm.shape[0] - 2 * MOE_BLOCK
        obuf[0] = jnp.zeros(obuf.shape[1:], F32)
        for i in range(2):
            dump = y_hbm.at[pl.ds(n_real + i * MOE_BLOCK, MOE_BLOCK)]
            pltpu.make_async_copy(obuf.at[0], dump, ssem.at[i]).start()
        for i in range(2):
            wait_scatter(i)
        for j in range(MOE_BLOCK):
            gather_copy(0, j, 0).start()

    @pl.when(b < total)
    def _():
        @pl.when(blk_first_ref[b] == 1)
        def _():
            wg_bf[...] = wg_ref[...].astype(BF16)
            wu_bf[...] = wu_ref[...].astype(BF16)
            wd_bf[...] = wd_ref[...].astype(BF16)

        wait_gather(slot)
        for j in range(MOE_BLOCK):
            gather_copy(b + 1, j, 1 - slot).start()
        hf = xbuf[slot] * (1.0 + vec_ref[_V_SCALE_A:_V_SCALE_A + 1, :]) \
            + vec_ref[_V_SHIFT_A:_V_SHIFT_A + 1, :]
        hb = hf.astype(BF16)
        gt = jnp.dot(hb, wg_bf[...], preferred_element_type=F32)
        up = jnp.dot(hb, wu_bf[...], preferred_element_type=F32)
        hid = gt * (1.0 / (1.0 + jnp.exp(-gt))) * up
        obuf[slot] = jnp.dot(hid.astype(BF16), wd_bf[...], preferred_element_type=F32)
        for j in range(MOE_BLOCK):
            scatter_copy(b, j, slot).start()

        @pl.when(b >= 1)
        def _():
            wait_scatter(1 - slot)

        @pl.when(b == total - 1)
        def _():
            wait_scatter(slot)
            wait_gather(1 - slot)


def _moe_dispatch(ids, n_experts):
    s, k = ids.shape
    a = s * k
    n_blocks = -(-a // MOE_BLOCK) + n_experts
    e_flat = ids.reshape(a)
    order = jnp.argsort(e_flat, stable=True).astype(jnp.int32)
    experts = jnp.arange(n_experts, dtype=jnp.int32)
    counts = jnp.sum((e_flat[:, None] == experts[None, :]).astype(jnp.int32), axis=0)
    starts = jnp.cumsum(counts) - counts
    nblk = (counts + MOE_BLOCK - 1) // MOE_BLOCK
    blk_end = jnp.cumsum(nblk)
    blk_start = blk_end - nblk
    total = blk_end[-1]
    bidx = jnp.arange(n_blocks, dtype=jnp.int32)
    used = bidx < total
    b_eff = jnp.minimum(bidx, total - 1)
    blk_e = jnp.sum((blk_end[None, :] <= b_eff[:, None]).astype(jnp.int32), axis=1)
    blk_e = jnp.minimum(blk_e, n_experts - 1)
    rank0 = (bidx - blk_start[blk_e]) * MOE_BLOCK
    blk_first = (used & (rank0 == 0)).astype(jnp.int32)
    lane = jnp.arange(MOE_BLOCK, dtype=jnp.int32)[None, :]
    rank = rank0[:, None] + lane
    valid = used[:, None] & (rank < counts[blk_e][:, None])
    src = jnp.clip(starts[blk_e][:, None] + rank, 0, a - 1)
    asg = order[src]
    slot_tok = jnp.where(valid, asg // k, 0)
    slot_tok = jnp.concatenate([slot_tok, jnp.zeros((1, MOE_BLOCK), jnp.int32)]).reshape(-1)
    dump = a + (bidx[:, None] % 2) * MOE_BLOCK + lane
    slot_dst = jnp.where(valid, (asg % k) * s + asg // k, dump).reshape(-1)
    return blk_e, total.reshape(1), blk_first, slot_tok, slot_dst, n_blocks


def _moe_experts(x2, vecs, ids, layer, w_gate, w_up, w_down):
    s, d = x2.shape
    n_experts, _, f = w_gate.shape[1:]
    k = ids.shape[1]
    blk_e, total, blk_first, slot_tok, slot_dst, n_blocks = _moe_dispatch(ids, n_experts)
    w_idx = lambda b, be, *_: (layer, be[b], 0, 0)
    return pl.pallas_call(
        _moe_kernel,
        out_shape=jax.ShapeDtypeStruct((s * k + 2 * MOE_BLOCK, d), F32),
        grid_spec=pltpu.PrefetchScalarGridSpec(
            num_scalar_prefetch=5,
            grid=(n_blocks,),
            in_specs=[pl.BlockSpec(memory_space=pl.ANY),
                      pl.BlockSpec((8, d), lambda b, *_: (0, 0)),
                      pl.BlockSpec((None, None, d, f), w_idx),
                      pl.BlockSpec((None, None, d, f), w_idx),
                      pl.BlockSpec((None, None, f, d), w_idx)],
            out_specs=pl.BlockSpec(memory_space=pl.ANY),
            scratch_shapes=[pltpu.VMEM((2, MOE_BLOCK, d), F32),
                            pltpu.VMEM((2, MOE_BLOCK, d), F32),
                            pltpu.VMEM((d, f), BF16),
                            pltpu.VMEM((d, f), BF16),
                            pltpu.VMEM((f, d), BF16),
                            pltpu.SemaphoreType.DMA((2,)),
                            pltpu.SemaphoreType.DMA((2,))]),
        compiler_params=_params(("arbitrary",)),
        name="moe_experts",
    )(blk_e, total, blk_first, slot_tok, slot_dst, x2, vecs, w_gate, w_up, w_down)


def _vec_pack(d, **rows):
    out = jnp.zeros((8, d), F32)
    for name, v in rows.items():
        out = out.at[name_to_row[name]].set(v)
    return out


name_to_row = {"gate": _V_GATE, "gain": _V_GAIN, "bias": _V_BIAS, "scale_a": _V_SCALE_A,
               "shift_a": _V_SHIFT_A, "scale_b": _V_SCALE_B, "shift_b": _V_SHIFT_B}


def kernel(x, c, ada_down, ada_up, ada_bias, ln_gain, ln_bias, a_w_qkv, a_w_o, kv_ada_down, kv_ada_up, kv_ada_bias, kv_w, b_w_q, b_w_o, moe_w_group, moe_b_group, moe_w_expert, moe_b_expert, moe_w_gate, moe_w_up, moe_w_down):
    batch, s, d = x.shape
    assert batch == 1
    depth = ada_down.shape[0]
    n_a = a_w_qkv.shape[0]
    alpha = (2 * depth) ** 0.25
    a_heads = a_w_o.shape[1] // HEAD_DIM
    a_width = a_heads * HEAD_DIM
    n_groups = moe_w_group.shape[-1]
    n_per_group = moe_w_expert.shape[-1]
    kv_heads = kv_w.shape[1] // (2 * HEAD_DIM)
    kv_width = kv_heads * HEAD_DIM
    b_heads = kv_heads * MOBA_REP
    n_blk = s // MOBA_BLOCK

    mods = _modulation(c, ada_down, ada_up, ada_bias).reshape(depth, 6, d)
    kv_mod = _modulation(c, kv_ada_down[None], kv_ada_up[None], kv_ada_bias[None]).reshape(2, d)

    x2 = x.reshape(s, d)
    hs = _prologue(x2, _vec_pack(d, scale_a=mods[0, 1], shift_a=mods[0, 0]))

    k_blocks = vt_blocks = k_mean = slope_lanes = None
    for l in range(depth):
        shift1, scale1, gate1, shift2, scale2, gate2 = [mods[l, i] for i in range(6)]
        if l < n_a:
            w_qkv = a_w_qkv[l].astype(BF16)
            outs, lses = [], []
            for g, (_, dil) in enumerate(A_GROUPS):
                qkv = _matmul(hs, w_qkv, BF16, n_off=g * 3 * a_width, n_out=3 * a_width, dil=dil)
                o_g, lse_g = _dilated_attention(qkv, dil, a_heads)
                outs.append(o_g)
                lses.append(lse_g)
            merged = _merge_groups(outs, lses, a_heads)
            mix = _matmul(merged, a_w_o[l].astype(BF16), F32)
        else:
            j = l - n_a
            q = _matmul(hs, b_w_q[j].astype(BF16), F32)
            o = _moba_attention(q, k_blocks, vt_blocks, k_mean, slope_lanes)
            mix = _matmul(o, b_w_o[j].astype(BF16), F32)

        wr = jnp.concatenate(
            [moe_w_group[l], jnp.moveaxis(moe_w_expert[l], 0, 1).reshape(d, n_groups * n_per_group),
             jnp.zeros((d, LANES - n_groups * (1 + n_per_group)), F32)], axis=1)
        br = jnp.concatenate(
            [moe_b_group[l], moe_b_expert[l].reshape(-1),
             jnp.zeros((LANES - n_groups * (1 + n_per_group),), F32)]).reshape(1, LANES)
        vec1 = _vec_pack(d, gate=gate1, gain=ln_gain[l, 0], bias=ln_bias[l, 0],
                         scale_a=scale2, shift_a=shift2)
        x2, route = _ln_router(x2, mix, vec1, wr, br, alpha, n_groups, n_per_group)
        ids = route[:, :MOE_TOPK].astype(jnp.int32)
        y2 = _moe_experts(x2, vec1, ids, l, moe_w_gate, moe_w_up, moe_w_down)

        last = l == depth - 1
        with_kv = l == n_a - 1 and n_a < depth
        rows = dict(gate=gate2, gain=ln_gain[l, 1], bias=ln_bias[l, 1])
        if not last:
            rows.update(scale_a=mods[l + 1, 1], shift_a=mods[l + 1, 0])
        if with_kv:
            rows.update(scale_b=kv_mod[1], shift_b=kv_mod[0])
        res = _ln_combine(x2, y2, route, _vec_pack(d, **rows), alpha, not last, with_kv)
        x2 = res[0]
        if with_kv:
            kv = _matmul(res[1], kv_w.astype(BF16), F32)
            k_f32 = kv[:, :kv_width]
            k_mean = _block_means(k_f32).reshape(n_blk, kv_heads, HEAD_DIM).transpose(1, 0, 2)
            k_blocks = k_f32.astype(BF16).reshape(n_blk, MOBA_BLOCK, kv_heads, HEAD_DIM) \
                .transpose(2, 0, 1, 3)
            vt_blocks = kv[:, kv_width:].astype(BF16).reshape(n_blk, MOBA_BLOCK, kv_heads, HEAD_DIM) \
                .transpose(2, 0, 3, 1)
            slopes = jnp.exp2(-8.0 * jnp.arange(1, b_heads + 1, dtype=F32) / b_heads)
            slope_lanes = jnp.repeat(slopes.reshape(kv_heads, MOBA_REP), MOBA_BLOCK, axis=1) \
                .reshape(kv_heads, 1, MOBA_REP * MOBA_BLOCK)
        if not last:
            hs = res[-1]
    return x2.reshape(batch, s, d)
```

```python
import functools
import math

import numpy as np
import jax
import jax.numpy as jnp
from jax import lax
from jax.experimental import pallas as pl
from jax.experimental.pallas import tpu as pltpu

F32 = jnp.float32
BF16 = jnp.bfloat16
U32 = jnp.uint32
HIGHEST = lax.Precision.HIGHEST

HEAD_DIM = 128
A_GROUPS = ((128, 1), (512, 4), (2048, 16))
A_BLOCK = 128
MOBA_BLOCK = 256
MOBA_TOPK = 3
MOBA_REP = 4
MOBA_UNROLL = 4
MOE_TOPK = 2
MOE_BLOCK = 128
LN_EPS = 1e-5
NEG = -1e30
LOG2E = math.log2(math.e)
LANES = 128
VMEM_LIMIT = 56 * 1024 * 1024

_NT = (((1,), (1,)), ((), ()))


def _params(sem, vmem=VMEM_LIMIT):
    return pltpu.CompilerParams(dimension_semantics=sem, vmem_limit_bytes=vmem)


def _mod_kernel(c_ref, wd_ref, wu_ref, b_ref, o_ref, t_ref):
    @pl.when(pl.program_id(1) == 0)
    def _():
        c = c_ref[...]
        sc = c * (1.0 / (1.0 + jnp.exp(-c)))
        t_ref[...] = jnp.dot(sc, wd_ref[...], preferred_element_type=F32, precision=HIGHEST)

    o_ref[...] = jnp.dot(t_ref[...], wu_ref[...], preferred_element_type=F32,
                         precision=HIGHEST) + b_ref[...]


def _modulation(c, w_down, w_up, b_up):
    n_l, d, r = w_down.shape
    n = w_up.shape[-1]
    tn = n // pl.cdiv(n, 6144)
    c8 = jnp.broadcast_to(c, (8, d))
    out = pl.pallas_call(
        _mod_kernel,
        out_shape=jax.ShapeDtypeStruct((n_l, 8, n), F32),
        grid=(n_l, n // tn),
        in_specs=[
            pl.BlockSpec((8, d), lambda l, j: (0, 0)),
            pl.BlockSpec((None, d, r), lambda l, j: (l, 0, 0)),
            pl.BlockSpec((None, r, tn), lambda l, j: (l, 0, j)),
            pl.BlockSpec((None, 1, tn), lambda l, j: (l, 0, j)),
        ],
        out_specs=pl.BlockSpec((None, 8, tn), lambda l, j: (l, 0, j)),
        scratch_shapes=[pltpu.VMEM((8, r), F32)],
        compiler_params=_params(("arbitrary", "arbitrary")),
        name="modulation",
    )(c8, w_down, w_up, b_up.reshape(n_l, 1, n))
    return out[:, 0, :]


def _mm_kernel(a_ref, b_ref, o_ref):
    o_ref[...] = jnp.dot(a_ref[...], b_ref[...],
                         preferred_element_type=F32).astype(o_ref.dtype)


def _matmul(a, b, out_dtype, n_off=0, n_out=None, dil=1, tm=512, tn=1024):
    m, k = a.shape
    n_out = b.shape[1] if n_out is None else n_out
    tm = min(tm, m // dil)
    tn = min(tn, n_out)
    off = n_off // tn
    bps = m // dil // tm
    return pl.pallas_call(
        _mm_kernel,
        out_shape=jax.ShapeDtypeStruct((m, n_out), out_dtype),
        grid=(n_out // tn, m // tm),
        in_specs=[
            pl.BlockSpec((tm, k), lambda j, i: (i % bps, i // bps)),
            pl.BlockSpec((k, tn), lambda j, i: (0, j + off)),
        ],
        out_specs=pl.BlockSpec((tm, tn), lambda j, i: (i, j)),
        compiler_params=_params(("parallel", "parallel")),
        name="matmul",
    )(a.reshape(m // dil, dil * k), b)


def _pack_halves(v):
    n = v.shape[1] // 2
    hi = lax.bitcast_convert_type(v[:, :n].astype(BF16).astype(F32), U32)
    lo = lax.bitcast_convert_type(v[:, n:].astype(BF16).astype(F32), U32)
    return (hi & jnp.uint32(0xFFFF0000)) | (lo >> 16)


def _unpack_halves(p):
    return (lax.bitcast_convert_type(p & jnp.uint32(0xFFFF0000), F32),
            lax.bitcast_convert_type(p << 16, F32))


def _layer_norm(v, gain, bias):
    mu = jnp.mean(v, axis=-1, keepdims=True)
    vc = v - mu
    var = jnp.mean(vc * vc, axis=-1, keepdims=True)
    return vc * lax.rsqrt(var + LN_EPS) * gain + bias


_V_GATE, _V_GAIN, _V_BIAS, _V_SCALE_A, _V_SHIFT_A, _V_SCALE_B, _V_SHIFT_B = range(7)


def _prologue_kernel(x_ref, vec_ref, o_ref):
    h = x_ref[...] * (1.0 + vec_ref[_V_SCALE_A:_V_SCALE_A + 1, :]) \
        + vec_ref[_V_SHIFT_A:_V_SHIFT_A + 1, :]
    o_ref[...] = h.astype(BF16)


def _prologue(x2, vecs, tm=256):
    s, d = x2.shape
    return pl.pallas_call(
        _prologue_kernel,
        out_shape=jax.ShapeDtypeStruct((s, d), BF16),
        grid=(s // tm,),
        in_specs=[pl.BlockSpec((tm, d), lambda i: (i, 0)),
                  pl.BlockSpec((8, d), lambda i: (0, 0))],
        out_specs=pl.BlockSpec((tm, d), lambda i: (i, 0)),
        compiler_params=_params(("parallel",)),
        name="prologue",
    )(x2, vecs)


def _ln_router_kernel(alpha, n_groups, n_per_group,
                      x_ref, mix_ref, vec_ref, wr_ref, br_ref, xo_ref, route_ref, hfp_ref):
    gate = vec_ref[_V_GATE:_V_GATE + 1, :]
    x1 = alpha * x_ref[...] + (1.0 + gate) * mix_ref[...].astype(F32)
    xn = _layer_norm(x1, vec_ref[_V_GAIN:_V_GAIN + 1, :], vec_ref[_V_BIAS:_V_BIAS + 1, :])
    xo_ref[...] = xn
    hf = xn * (1.0 + vec_ref[_V_SCALE_A:_V_SCALE_A + 1, :]) + vec_ref[_V_SHIFT_A:_V_SHIFT_A + 1, :]
    hfp_ref[...] = _pack_halves(hf)
    logits = jnp.dot(hf, wr_ref[...], preferred_element_type=F32, precision=HIGHEST) + br_ref[...]
    lane = lax.broadcasted_iota(jnp.int32, logits.shape, 1).astype(F32)
    ninf = -jnp.inf
    big = float(LANES)
    is_g = lane < n_groups
    gl = jnp.where(is_g, logits, ninf)
    gmax = jnp.max(gl, axis=-1, keepdims=True)
    gsel = jnp.min(jnp.where(gl == gmax, lane, big), axis=-1, keepdims=True)
    p_group = 1.0 / jnp.sum(jnp.where(is_g, jnp.exp(logits - gmax), 0.0), axis=-1, keepdims=True)
    lo = n_groups + gsel * n_per_group
    el = jnp.where((lane >= lo) & (lane < lo + n_per_group), logits, ninf)
    m1 = jnp.max(el, axis=-1, keepdims=True)
    i1 = jnp.min(jnp.where(el == m1, lane, big), axis=-1, keepdims=True)
    el2 = jnp.where(lane == i1, ninf, el)
    m2 = jnp.max(el2, axis=-1, keepdims=True)
    i2 = jnp.min(jnp.where(el2 == m2, lane, big), axis=-1, keepdims=True)
    e21 = jnp.exp(m2 - m1)
    inv = 1.0 / (1.0 + e21)
    w1 = p_group * inv
    w2 = p_group * (e21 * inv)
    route_ref[...] = jnp.where(lane == 0.0, i1 - n_groups,
                     jnp.where(lane == 1.0, i2 - n_groups,
                     jnp.where(lane == 2.0, w1,
                     jnp.where(lane == 3.0, w2, 0.0))))


def _ln_router(x2, mix, vecs, wr, br, alpha, n_groups, n_per_group, tm=256):
    s, d = x2.shape
    return pl.pallas_call(
        functools.partial(_ln_router_kernel, alpha, n_groups, n_per_group),
        out_shape=[jax.ShapeDtypeStruct((s, d), F32),
                   jax.ShapeDtypeStruct((s, LANES), F32),
                   jax.ShapeDtypeStruct((s, d // 2), U32)],
        grid=(s // tm,),
        in_specs=[pl.BlockSpec((tm, d), lambda i: (i, 0)),
                  pl.BlockSpec((tm, d), lambda i: (i, 0)),
                  pl.BlockSpec((8, d), lambda i: (0, 0)),
                  pl.BlockSpec((d, LANES), lambda i: (0, 0)),
                  pl.BlockSpec((1, LANES), lambda i: (0, 0))],
        out_specs=[pl.BlockSpec((tm, d), lambda i: (i, 0)),
                   pl.BlockSpec((tm, LANES), lambda i: (i, 0)),
                   pl.BlockSpec((tm, d // 2), lambda i: (i, 0))],
        compiler_params=_params(("parallel",)),
        name="ln_router",
    )(x2, mix, vecs, wr, br)


def _ln_combine_kernel(alpha, with_next, with_kv, x_ref, y0_ref, y1_ref, route_ref, vec_ref, *refs):
    w1, w2 = route_ref[:, 2:3], route_ref[:, 3:4]
    a0, b0 = _unpack_halves(y0_ref[...])
    a1, b1 = _unpack_halves(y1_ref[...])
    ffn = jnp.concatenate([a0 * w1 + a1 * w2, b0 * w1 + b1 * w2], axis=1)
    gate = vec_ref[_V_GATE:_V_GATE + 1, :]
    x1 = alpha * x_ref[...] + (1.0 + gate) * ffn
    xn = _layer_norm(x1, vec_ref[_V_GAIN:_V_GAIN + 1, :], vec_ref[_V_BIAS:_V_BIAS + 1, :])
    refs[0][...] = xn
    k = 1
    if with_kv:
        hk = xn * (1.0 + vec_ref[_V_SCALE_B:_V_SCALE_B + 1, :]) + vec_ref[_V_SHIFT_B:_V_SHIFT_B + 1, :]
        refs[k][...] = hk.astype(BF16)
        k += 1
    if with_next:
        h = xn * (1.0 + vec_ref[_V_SCALE_A:_V_SCALE_A + 1, :]) + vec_ref[_V_SHIFT_A:_V_SHIFT_A + 1, :]
        refs[k][...] = h.astype(BF16)


def _ln_combine(x2, y2, route, vecs, alpha, with_next, with_kv, tm=256):
    s, d = x2.shape
    row = lambda i: (i, 0)
    n_bf = int(with_kv) + int(with_next)
    shapes = [jax.ShapeDtypeStruct((s, d), F32)] + [jax.ShapeDtypeStruct((s, d), BF16)] * n_bf
    specs = [pl.BlockSpec((tm, d), row)] * (1 + n_bf)
    return pl.pallas_call(
        functools.partial(_ln_combine_kernel, alpha, with_next, with_kv),
        out_shape=shapes,
        grid=(s // tm,),
        in_specs=[pl.BlockSpec((tm, d), row),
                  pl.BlockSpec((tm, d // 2), row),
                  pl.BlockSpec((tm, d // 2), lambda i: (i + s // tm, 0)),
                  pl.BlockSpec((tm, LANES), row),
                  pl.BlockSpec((8, d), lambda i: (0, 0))],
        out_specs=specs,
        compiler_params=_params(("parallel",)),
        name="ln_combine",
    )(x2, y2, y2, route, vecs)


def _dilated_kernel(dil, blocks_per_stream, slopes, q_ref, kp_ref, kc_ref, vp_ref, vc_ref,
                    o_ref, lse_ref):
    nb = pl.program_id(0) % blocks_per_stream
    has_prev = nb > 0
    qi = lax.broadcasted_iota(jnp.int32, (A_BLOCK, A_BLOCK), 0)
    kj = lax.broadcasted_iota(jnp.int32, (A_BLOCK, A_BLOCK), 1)
    dist_prev = jnp.where((kj >= qi) & has_prev, ((qi + A_BLOCK - kj) * dil).astype(F32), -NEG)
    dist_cur = jnp.where(kj <= qi, ((qi - kj) * dil).astype(F32), -NEG)
    sqrt_e = math.sqrt(HEAD_DIM)
    lane = lax.broadcasted_iota(jnp.int32, (A_BLOCK, LANES), 1)
    ones = jnp.ones((A_BLOCK, LANES), BF16)
    n_heads = len(slopes)

    def scores(h):
        hs = slice(h * HEAD_DIM, (h + 1) * HEAD_DIM)
        q = q_ref[:, hs]
        return (lax.dot_general(q, kp_ref[:, hs], _NT, preferred_element_type=F32),
                lax.dot_general(q, kc_ref[:, hs], _NT, preferred_element_type=F32))

    ahead = 6
    pending = {h: scores(h) for h in range(min(ahead, n_heads))}
    lse_all = jnp.zeros((A_BLOCK, LANES), F32)
    for h, slope in enumerate(slopes):
        hs = slice(h * HEAD_DIM, (h + 1) * HEAD_DIM)
        s_p, s_c = pending.pop(h)
        t_p = s_p - (slope * sqrt_e) * dist_prev
        t_c = s_c - (slope * sqrt_e) * dist_cur
        t_max = jnp.max(jnp.maximum(t_p, t_c), axis=-1, keepdims=True)
        p_p = jnp.exp2((t_p - t_max) * (LOG2E / sqrt_e)).astype(BF16)
        p_c = jnp.exp2((t_c - t_max) * (LOG2E / sqrt_e)).astype(BF16)
        if h + ahead < n_heads:
            pending[h + ahead] = scores(h + ahead)
        o = jnp.dot(p_p, vp_ref[:, hs], preferred_element_type=F32) \
            + jnp.dot(p_c, vc_ref[:, hs], preferred_element_type=F32)
        l = jnp.dot(p_p, ones, preferred_element_type=F32) \
            + jnp.dot(p_c, ones, preferred_element_type=F32)
        o_ref[:, hs] = o * (1.0 / l)
        lse_all = jnp.where(lane == h, t_max * (1.0 / sqrt_e) + jnp.log(l), lse_all)
    lse_ref[...] = lse_all


def _dilated_attention(qkv, dil, n_heads):
    s = qkv.shape[0]
    w = n_heads * HEAD_DIM
    n = s // dil
    bps = n // A_BLOCK
    slopes = [float(v) for v in np.exp2(-8.0 * np.arange(1, n_heads + 1, dtype=np.float32) / n_heads)]
    prev = lambda b: (jnp.maximum(b - 1, 0), 1)
    o, lse = pl.pallas_call(
        functools.partial(_dilated_kernel, dil, bps, slopes),
        out_shape=[jax.ShapeDtypeStruct((n, dil * w), F32),
                   jax.ShapeDtypeStruct((n, dil * LANES), F32)],
        grid=(s // A_BLOCK,),
        in_specs=[pl.BlockSpec((A_BLOCK, w), lambda b: (b, 0)),
                  pl.BlockSpec((A_BLOCK, w), prev),
                  pl.BlockSpec((A_BLOCK, w), lambda b: (b, 1)),
                  pl.BlockSpec((A_BLOCK, w), lambda b: (jnp.maximum(b - 1, 0), 2)),
                  pl.BlockSpec((A_BLOCK, w), lambda b: (b, 2))],
        out_specs=[pl.BlockSpec((A_BLOCK, w), lambda b: (b % bps, b // bps)),
                   pl.BlockSpec((A_BLOCK, LANES), lambda b: (b % bps, b // bps))],
        compiler_params=_params(("parallel",)),
        name="dilated_attention",
    )(qkv, qkv, qkv, qkv, qkv)
    return o, lse


def _merge_kernel(n_heads, *refs):
    n_g = len(A_GROUPS)
    o_refs, l_refs = refs[:n_g], refs[n_g:2 * n_g]
    out_ref = refs[2 * n_g]
    stages = iter(refs[2 * n_g + 1:])
    tm = out_ref.shape[0]
    o_nat, l_nat = [], []
    for (_, dil), o_ref, l_ref in zip(A_GROUPS, o_refs, l_refs):
        if dil == 1:
            o_nat.append(lambda h, o_ref=o_ref: o_ref[:, h * HEAD_DIM:(h + 1) * HEAD_DIM])
            l_nat.append(l_ref[...])
            continue
        stage = next(stages)
        rows = tm // dil
        w = o_ref.shape[1] // dil
        for r in range(dil):
            for h in range(n_heads):
                stage[h, pl.ds(r, rows, stride=dil), :] = \
                    o_ref[:, r * w + h * HEAD_DIM:r * w + (h + 1) * HEAD_DIM]
            stage[n_heads, pl.ds(r, rows, stride=dil), :] = l_ref[:, r * LANES:(r + 1) * LANES]
        o_nat.append(lambda h, stage=stage: stage[h])
        l_nat.append(stage[n_heads])
    m = functools.reduce(jnp.maximum, l_nat)
    es = [jnp.exp(l - m) for l in l_nat]
    inv = 1.0 / functools.reduce(lambda a, b: a + b, es)
    ws = [e * inv for e in es]
    for h in range(n_heads):
        acc = functools.reduce(lambda a, b: a + b,
                               [o(h) * w_g[:, h:h + 1] for o, w_g in zip(o_nat, ws)])
        out_ref[:, h * HEAD_DIM:(h + 1) * HEAD_DIM] = acc.astype(BF16)


def _merge_groups(os_, lses, n_heads, tm=256):
    w = n_heads * HEAD_DIM
    s = os_[0].shape[0] * A_GROUPS[0][1]
    row = lambda i: (i, 0)
    dils = [dil for _, dil in A_GROUPS]
    return pl.pallas_call(
        functools.partial(_merge_kernel, n_heads),
        out_shape=jax.ShapeDtypeStruct((s, w), BF16),
        grid=(s // tm,),
        in_specs=[pl.BlockSpec((tm // dil, dil * w), row) for dil in dils]
        + [pl.BlockSpec((tm // dil, dil * LANES), row) for dil in dils],
        out_specs=pl.BlockSpec((tm, w), row),
        scratch_shapes=[pltpu.VMEM((n_heads + 1, tm, LANES), F32) for dil in dils if dil > 1],
        compiler_params=_params(("parallel",)),
        name="merge_groups",
    )(*os_, *lses)


def _kmean_kernel(k_ref, o_ref):
    o_ref[...] = jnp.mean(k_ref[...], axis=0, keepdims=True)


def _block_means(k):
    s, kw = k.shape
    nb = s // MOBA_BLOCK
    return pl.pallas_call(
        _kmean_kernel,
        out_shape=jax.ShapeDtypeStruct((nb, 1, kw), F32),
        grid=(nb,),
        in_specs=[pl.BlockSpec((MOBA_BLOCK, kw), lambda i: (i, 0))],
        out_specs=pl.BlockSpec((None, 1, kw), lambda i: (i, 0, 0)),
        compiler_params=_params(("parallel",)),
        name="block_means",
    )(k)


def _moba_kernel(q_ref, k_ref, vt_ref, kmean_ref, slope_ref, o_ref,
                 qs_ref, sel_ref, bias_ref, m_ref, l_ref, acc_ref):
    qb = pl.program_id(1)
    n_blk = kmean_ref.shape[0]
    rows = MOBA_REP * MOBA_BLOCK
    n_chunks = rows // LANES
    slope2 = slope_ref[...] * LOG2E

    @pl.when(qb == 0)
    def _():
        lane_t = (lax.broadcasted_iota(jnp.int32, (1, rows), 1) % MOBA_BLOCK).astype(F32)
        sub = lax.broadcasted_iota(jnp.int32, (MOBA_BLOCK, 1), 0).astype(F32)
        bias_ref[...] = slope2 * (lane_t - sub)

    q = q_ref[...]
    q4 = jnp.concatenate([q[:, r * HEAD_DIM:(r + 1) * HEAD_DIM] for r in range(MOBA_REP)], axis=0)
    gate = lax.dot_general(kmean_ref[...], q4, _NT, preferred_element_type=F32, precision=HIGHEST)
    blk = lax.broadcasted_iota(jnp.int32, (n_blk, rows), 0).astype(F32)
    ninf = -jnp.inf
    gate = jnp.where(blk < qb.astype(F32), gate, ninf)
    sel = jnp.zeros((n_blk, rows), F32)
    for _ in range(MOBA_TOPK):
        top = jnp.max(gate, axis=0, keepdims=True)
        idx = jnp.min(jnp.where(gate == top, blk, float(n_blk)), axis=0, keepdims=True)
        hit = (blk == idx) & (top > ninf)
        sel = jnp.where(hit, 1.0, sel)
        gate = jnp.where(hit, ninf, gate)
    sel_ref[...] = sel
    qs_ref[...] = (q4 * (LOG2E / math.sqrt(HEAD_DIM))).astype(BF16)

    kk = lax.broadcasted_iota(jnp.int32, (MOBA_BLOCK, LANES), 0)
    tt = lax.broadcasted_iota(jnp.int32, (MOBA_BLOCK, LANES), 1)

    def run_blocks(blocks, own):
        k_b = [k_ref[n] for n in blocks]
        vt_b = [vt_ref[n] for n in blocks]
        sel_b = [None if own else sel_ref[pl.ds(n, 1), :] for n in blocks]
        bt_b = [None if own else slope2 * ((n - qb) * MOBA_BLOCK).astype(F32) for n in blocks]
        tasks = [(bi, c) for bi in range(len(blocks)) for c in range(n_chunks)]

        def score(t):
            bi, c = tasks[t]
            return lax.dot_general(k_b[bi], qs_ref[c * LANES:(c + 1) * LANES, :], _NT,
                                   preferred_element_type=F32)

        ahead = 2
        pending = {t: score(t) for t in range(min(ahead, len(tasks)))}
        for t, (bi, c) in enumerate(tasks):
            cs = slice(c * LANES, (c + 1) * LANES)
            u = pending.pop(t) - bias_ref[:, cs]
            if own:
                u = jnp.where(kk <= tt + (c * LANES) % MOBA_BLOCK, u, NEG)
                m_new = jnp.max(u, axis=0, keepdims=True)
                p = jnp.exp2(u - m_new)
                l_new = jnp.sum(p, axis=0, keepdims=True)
            else:
                picked = sel_b[bi][:, cs] > 0.0
                bt = bt_b[bi][:, cs]
                m_old = m_ref[:, cs]
                m_new = jnp.maximum(
                    m_old, jnp.where(picked, jnp.max(u, axis=0, keepdims=True) + bt, NEG))
                p = jnp.exp2(u - jnp.where(picked, m_new - bt, -NEG))
                alpha = jnp.exp2(m_old - m_new)
                l_new = alpha * l_ref[:, cs] + jnp.sum(p, axis=0, keepdims=True)
            m_ref[:, cs] = m_new
            l_ref[:, cs] = l_new
            if t + ahead < len(tasks):
                pending[t + ahead] = score(t + ahead)
            pv = jnp.dot(vt_b[bi], p.astype(BF16), preferred_element_type=F32)
            acc_ref[:, cs] = pv if own else alpha * acc_ref[:, cs] + pv

    run_blocks([qb], own=True)

    def body(i, carry):
        run_blocks([i * MOBA_UNROLL + j for j in range(MOBA_UNROLL)], own=False)
        return carry

    lax.fori_loop(0, (qb + MOBA_UNROLL - 1) // MOBA_UNROLL, body, 0)
    o_t = acc_ref[...] * (1.0 / l_ref[...])
    for r in range(MOBA_REP):
        o_ref[:, r * HEAD_DIM:(r + 1) * HEAD_DIM] = \
            o_t[:, r * MOBA_BLOCK:(r + 1) * MOBA_BLOCK].T.astype(BF16)


def _moba_attention(q, k_blocks, vt_blocks, k_mean, slope_lanes):
    s = q.shape[0]
    g, nb = k_blocks.shape[:2]
    assert nb % MOBA_UNROLL == 0
    rows = MOBA_REP * MOBA_BLOCK
    qw = MOBA_REP * HEAD_DIM
    return pl.pallas_call(
        _moba_kernel,
        out_shape=jax.ShapeDtypeStruct((s, g * qw), BF16),
        grid=(g, nb),
        in_specs=[pl.BlockSpec((MOBA_BLOCK, qw), lambda gi, qb: (qb, gi)),
                  pl.BlockSpec((None, nb, MOBA_BLOCK, HEAD_DIM), lambda gi, qb: (gi, 0, 0, 0)),
                  pl.BlockSpec((None, nb, HEAD_DIM, MOBA_BLOCK), lambda gi, qb: (gi, 0, 0, 0)),
                  pl.BlockSpec((None, nb, HEAD_DIM), lambda gi, qb: (gi, 0, 0)),
                  pl.BlockSpec((None, 1, rows), lambda gi, qb: (gi, 0, 0))],
        out_specs=pl.BlockSpec((MOBA_BLOCK, qw), lambda gi, qb: (qb, gi)),
        scratch_shapes=[pltpu.VMEM((rows, HEAD_DIM), BF16),
                        pltpu.VMEM((nb, rows), F32),
                        pltpu.VMEM((MOBA_BLOCK, rows), F32),
                        pltpu.VMEM((1, rows), F32),
                        pltpu.VMEM((1, rows), F32),
                        pltpu.VMEM((HEAD_DIM, rows), F32)],
        compiler_params=_params(("parallel", "arbitrary")),
        name="moba_attention",
    )(q, k_blocks, vt_blocks, k_mean, slope_lanes)


def _moe_kernel(blk_e_ref, total_ref, blk_first_ref, slot_tok_ref, slot_dst_ref,
                x_hbm, wg_ref, wu_ref, wd_ref, y_hbm,
                xbuf, obuf, wg_bf, wu_bf, wd_bf, gsem, ssem):
    b = pl.program_id(0)
    total = total_ref[0]
    slot = b % 2

    def gather_copy(blk, j, buf_slot):
        tok = slot_tok_ref[blk * MOE_BLOCK + j]
        return pltpu.make_async_copy(x_hbm.at[pl.ds(tok, 1)], xbuf.at[buf_slot, pl.ds(j, 1)],
                                     gsem.at[buf_slot])

    def scatter_copy(blk, j, buf_slot):
        dst = slot_dst_ref[blk * MOE_BLOCK + j]
        return pltpu.make_async_copy(obuf.at[buf_slot, pl.ds(j, 1)], y_hbm.at[pl.ds(dst, 1)],
                                     ssem.at[buf_slot])

    def wait_gather(buf_slot):
        pltpu.make_async_copy(x_hbm.at[pl.ds(0, MOE_BLOCK)], xbuf.at[buf_slot],
                              gsem.at[buf_slot]).wait()

    def wait_scatter(buf_slot):
        pltpu.make_async_copy(obuf.at[buf_slot], y_hbm.at[pl.ds(0, MOE_BLOCK)],
                              ssem.at[buf_slot]).wait()

    @pl.when(b == 0)
    def _():
        n_real = y_hbm.shape[0] - 2 * MOE_BLOCK
        obuf[0] = jnp.zeros(obuf.shape[1:], U32)
        for i in range(2):
            dump = y_hbm.at[pl.ds(n_real + i * MOE_BLOCK, MOE_BLOCK)]
            pltpu.make_async_copy(obuf.at[0], dump, ssem.at[i]).start()
        for i in range(2):
            wait_scatter(i)
        for j in range(MOE_BLOCK):
            gather_copy(0, j, 0).start()

    @pl.when(b < total)
    def _():
        @pl.when(blk_first_ref[b] == 1)
        def _():
            wg_bf[...] = wg_ref[...].astype(BF16)
            wu_bf[...] = wu_ref[...].astype(BF16)
            wd_bf[...] = wd_ref[...].astype(BF16)

        wait_gather(slot)
        for j in range(MOE_BLOCK):
            gather_copy(b + 1, j, 1 - slot).start()
        h_lo, h_hi = _unpack_halves(xbuf[slot])
        hb = jnp.concatenate([h_lo.astype(BF16), h_hi.astype(BF16)], axis=1)
        gt = jnp.dot(hb, wg_bf[...], preferred_element_type=F32)
        up = jnp.dot(hb, wu_bf[...], preferred_element_type=F32)
        hid = gt * (1.0 / (1.0 + jnp.exp(-gt))) * up
        obuf[slot] = _pack_halves(jnp.dot(hid.astype(BF16), wd_bf[...],
                                          preferred_element_type=F32))
        for j in range(MOE_BLOCK):
            scatter_copy(b, j, slot).start()

        @pl.when(b >= 1)
        def _():
            wait_scatter(1 - slot)

        @pl.when(b == total - 1)
        def _():
            wait_scatter(slot)
            wait_gather(1 - slot)


def _moe_dispatch(ids, n_experts):
    s, k = ids.shape
    a = s * k
    n_blocks = -(-a // MOE_BLOCK) + n_experts
    e_flat = ids.reshape(a)
    order = jnp.argsort(e_flat, stable=True).astype(jnp.int32)
    experts = jnp.arange(n_experts, dtype=jnp.int32)
    counts = jnp.sum((e_flat[:, None] == experts[None, :]).astype(jnp.int32), axis=0)
    starts = jnp.cumsum(counts) - counts
    nblk = (counts + MOE_BLOCK - 1) // MOE_BLOCK
    blk_end = jnp.cumsum(nblk)
    blk_start = blk_end - nblk
    total = blk_end[-1]
    bidx = jnp.arange(n_blocks, dtype=jnp.int32)
    used = bidx < total
    b_eff = jnp.minimum(bidx, total - 1)
    blk_e = jnp.sum((blk_end[None, :] <= b_eff[:, None]).astype(jnp.int32), axis=1)
    blk_e = jnp.minimum(blk_e, n_experts - 1)
    rank0 = (bidx - blk_start[blk_e]) * MOE_BLOCK
    blk_first = (used & (rank0 == 0)).astype(jnp.int32)
    lane = jnp.arange(MOE_BLOCK, dtype=jnp.int32)[None, :]
    rank = rank0[:, None] + lane
    valid = used[:, None] & (rank < counts[blk_e][:, None])
    src = jnp.clip(starts[blk_e][:, None] + rank, 0, a - 1)
    asg = order[src]
    slot_tok = jnp.where(valid, asg // k, 0)
    slot_tok = jnp.concatenate([slot_tok, jnp.zeros((1, MOE_BLOCK), jnp.int32)]).reshape(-1)
    dump = a + (bidx[:, None] % 2) * MOE_BLOCK + lane
    slot_dst = jnp.where(valid, (asg % k) * s + asg // k, dump).reshape(-1)
    return blk_e, total.reshape(1), blk_first, slot_tok, slot_dst, n_blocks


def _moe_experts(hfp, ids, layer, w_gate, w_up, w_down):
    s, dp = hfp.shape
    d = 2 * dp
    n_experts, _, f = w_gate.shape[1:]
    k = ids.shape[1]
    blk_e, total, blk_first, slot_tok, slot_dst, n_blocks = _moe_dispatch(ids, n_experts)
    w_idx = lambda b, be, *_: (layer, be[b], 0, 0)
    return pl.pallas_call(
        _moe_kernel,
        out_shape=jax.ShapeDtypeStruct((s * k + 2 * MOE_BLOCK, dp), U32),
        grid_spec=pltpu.PrefetchScalarGridSpec(
            num_scalar_prefetch=5,
            grid=(n_blocks,),
            in_specs=[pl.BlockSpec(memory_space=pl.ANY),
                      pl.BlockSpec((None, None, d, f), w_idx),
                      pl.BlockSpec((None, None, d, f), w_idx),
                      pl.BlockSpec((None, None, f, d), w_idx)],
            out_specs=pl.BlockSpec(memory_space=pl.ANY),
            scratch_shapes=[pltpu.VMEM((2, MOE_BLOCK, dp), U32),
                            pltpu.VMEM((2, MOE_BLOCK, dp), U32),
                            pltpu.VMEM((d, f), BF16),
                            pltpu.VMEM((d, f), BF16),
                            pltpu.VMEM((f, d), BF16),
                            pltpu.SemaphoreType.DMA((2,)),
                            pltpu.SemaphoreType.DMA((2,))]),
        compiler_params=_params(("arbitrary",)),
        name="moe_experts",
    )(blk_e, total, blk_first, slot_tok, slot_dst, hfp, w_gate, w_up, w_down)


def _vec_pack(d, **rows):
    out = jnp.zeros((8, d), F32)
    for name, v in rows.items():
        out = out.at[name_to_row[name]].set(v)
    return out


name_to_row = {"gate": _V_GATE, "gain": _V_GAIN, "bias": _V_BIAS, "scale_a": _V_SCALE_A,
               "shift_a": _V_SHIFT_A, "scale_b": _V_SCALE_B, "shift_b": _V_SHIFT_B}


def kernel(x, c, ada_down, ada_up, ada_bias, ln_gain, ln_bias, a_w_qkv, a_w_o, kv_ada_down, kv_ada_up, kv_ada_bias, kv_w, b_w_q, b_w_o, moe_w_group, moe_b_group, moe_w_expert, moe_b_expert, moe_w_gate, moe_w_up, moe_w_down):
    batch, s, d = x.shape
    assert batch == 1
    depth = ada_down.shape[0]
    n_a = a_w_qkv.shape[0]
    alpha = (2 * depth) ** 0.25
    a_heads = a_w_o.shape[1] // HEAD_DIM
    a_width = a_heads * HEAD_DIM
    n_groups = moe_w_group.shape[-1]
    n_per_group = moe_w_expert.shape[-1]
    kv_heads = kv_w.shape[1] // (2 * HEAD_DIM)
    kv_width = kv_heads * HEAD_DIM
    b_heads = kv_heads * MOBA_REP
    n_blk = s // MOBA_BLOCK

    mods = _modulation(c, ada_down, ada_up, ada_bias).reshape(depth, 6, d)
    kv_mod = _modulation(c, kv_ada_down[None], kv_ada_up[None], kv_ada_bias[None]).reshape(2, d)

    x2 = x.reshape(s, d)
    hs = _prologue(x2, _vec_pack(d, scale_a=mods[0, 1], shift_a=mods[0, 0]))

    k_blocks = vt_blocks = k_mean = slope_lanes = None
    for l in range(depth):
        shift1, scale1, gate1, shift2, scale2, gate2 = [mods[l, i] for i in range(6)]
        if l < n_a:
            w_qkv = a_w_qkv[l].astype(BF16)
            outs, lses = [], []
            for g, (_, dil) in enumerate(A_GROUPS):
                qkv = _matmul(hs, w_qkv, BF16, n_off=g * 3 * a_width, n_out=3 * a_width, dil=dil)
                o_g, lse_g = _dilated_attention(qkv, dil, a_heads)
                outs.append(o_g)
                lses.append(lse_g)
            merged = _merge_groups(outs, lses, a_heads)
            mix = _matmul(merged, a_w_o[l].astype(BF16), BF16)
        else:
            j = l - n_a
            q = _matmul(hs, b_w_q[j].astype(BF16), F32)
            o = _moba_attention(q, k_blocks, vt_blocks, k_mean, slope_lanes)
            mix = _matmul(o, b_w_o[j].astype(BF16), BF16)

        wr = jnp.concatenate(
            [moe_w_group[l], jnp.moveaxis(moe_w_expert[l], 0, 1).reshape(d, n_groups * n_per_group),
             jnp.zeros((d, LANES - n_groups * (1 + n_per_group)), F32)], axis=1)
        br = jnp.concatenate(
            [moe_b_group[l], moe_b_expert[l].reshape(-1),
             jnp.zeros((LANES - n_groups * (1 + n_per_group),), F32)]).reshape(1, LANES)
        vec1 = _vec_pack(d, gate=gate1, gain=ln_gain[l, 0], bias=ln_bias[l, 0],
                         scale_a=scale2, shift_a=shift2)
        x2, route, hfp = _ln_router(x2, mix, vec1, wr, br, alpha, n_groups, n_per_group)
        ids = route[:, :MOE_TOPK].astype(jnp.int32)
        y2 = _moe_experts(hfp, ids, l, moe_w_gate, moe_w_up, moe_w_down)

        last = l == depth - 1
        with_kv = l == n_a - 1 and n_a < depth
        rows = dict(gate=gate2, gain=ln_gain[l, 1], bias=ln_bias[l, 1])
        if not last:
            rows.update(scale_a=mods[l + 1, 1], shift_a=mods[l + 1, 0])
        if with_kv:
            rows.update(scale_b=kv_mod[1], shift_b=kv_mod[0])
        res = _ln_combine(x2, y2, route, _vec_pack(d, **rows), alpha, not last, with_kv)
        x2 = res[0]
        if with_kv:
            kv = _matmul(res[1], kv_w.astype(BF16), F32)
            k_f32 = kv[:, :kv_width]
            k_mean = _block_means(k_f32).reshape(n_blk, kv_heads, HEAD_DIM).transpose(1, 0, 2)
            k_blocks = k_f32.astype(BF16).reshape(n_blk, MOBA_BLOCK, kv_heads, HEAD_DIM) \
                .transpose(2, 0, 1, 3)
            vt_blocks = kv[:, kv_width:].astype(BF16).reshape(n_blk, MOBA_BLOCK, kv_heads, HEAD_DIM) \
                .transpose(2, 0, 3, 1)
            slopes = jnp.exp2(-8.0 * jnp.arange(1, b_heads + 1, dtype=F32) / b_heads)
            slope_lanes = jnp.repeat(slopes.reshape(kv_heads, MOBA_REP), MOBA_BLOCK, axis=1) \
                .reshape(kv_heads, 1, MOBA_REP * MOBA_BLOCK)
        if not last:
            hs = res[-1]
    return x2.reshape(batch, s, d)
```

```python
import functools
import math

import numpy as np
import jax
import jax.numpy as jnp
from jax import lax
from jax.experimental import pallas as pl
from jax.experimental.pallas import tpu as pltpu

F32 = jnp.float32
BF16 = jnp.bfloat16
U32 = jnp.uint32
HIGHEST = lax.Precision.HIGHEST

HEAD_DIM = 128
A_GROUPS = ((128, 1), (512, 4), (2048, 16))
A_BLOCK = 128
MOBA_BLOCK = 256
MOBA_TOPK = 3
MOBA_REP = 4
MOBA_UNROLL = 4
MOE_TOPK = 2
MOE_BLOCK = 128
LN_EPS = 1e-5
NEG = -1e30
LOG2E = math.log2(math.e)
LANES = 128
VMEM_LIMIT = 56 * 1024 * 1024

_NT = (((1,), (1,)), ((), ()))


def _params(sem, vmem=VMEM_LIMIT):
    return pltpu.CompilerParams(dimension_semantics=sem, vmem_limit_bytes=vmem)


def _mod_kernel(c_ref, wd_ref, wu_ref, b_ref, o_ref, t_ref):
    @pl.when(pl.program_id(1) == 0)
    def _():
        c = c_ref[...]
        sc = c * (1.0 / (1.0 + jnp.exp(-c)))
        t_ref[...] = jnp.dot(sc, wd_ref[...], preferred_element_type=F32, precision=HIGHEST)

    o_ref[...] = jnp.dot(t_ref[...], wu_ref[...], preferred_element_type=F32,
                         precision=HIGHEST) + b_ref[...]


def _modulation(c, w_down, w_up, b_up):
    n_l, d, r = w_down.shape
    n = w_up.shape[-1]
    tn = n // pl.cdiv(n, 6144)
    c8 = jnp.broadcast_to(c, (8, d))
    out = pl.pallas_call(
        _mod_kernel,
        out_shape=jax.ShapeDtypeStruct((n_l, 8, n), F32),
        grid=(n_l, n // tn),
        in_specs=[
            pl.BlockSpec((8, d), lambda l, j: (0, 0)),
            pl.BlockSpec((None, d, r), lambda l, j: (l, 0, 0)),
            pl.BlockSpec((None, r, tn), lambda l, j: (l, 0, j)),
            pl.BlockSpec((None, 1, tn), lambda l, j: (l, 0, j)),
        ],
        out_specs=pl.BlockSpec((None, 8, tn), lambda l, j: (l, 0, j)),
        scratch_shapes=[pltpu.VMEM((8, r), F32)],
        compiler_params=_params(("arbitrary", "arbitrary")),
        name="modulation",
    )(c8, w_down, w_up, b_up.reshape(n_l, 1, n))
    return out[:, 0, :]


def _mm_kernel(a_ref, b_ref, o_ref, b_bf_ref):
    @pl.when(pl.program_id(1) == 0)
    def _():
        b_bf_ref[...] = b_ref[...].astype(BF16)

    o_ref[...] = jnp.dot(a_ref[...], b_bf_ref[...],
                         preferred_element_type=F32).astype(o_ref.dtype)


def _matmul(a, b, layer, out_dtype, n_off=0, n_out=None, dil=1, tm=1024, tn=512):
    m, k = a.shape
    n_out = b.shape[2] if n_out is None else n_out
    tm = min(tm, m // dil)
    tn = min(tn, n_out)
    off = n_off // tn
    bps = m // dil // tm
    return pl.pallas_call(
        _mm_kernel,
        out_shape=jax.ShapeDtypeStruct((m, n_out), out_dtype),
        grid=(n_out // tn, m // tm),
        in_specs=[
            pl.BlockSpec((tm, k), lambda j, i: (i % bps, i // bps)),
            pl.BlockSpec((None, k, tn), lambda j, i: (layer, 0, j + off)),
        ],
        out_specs=pl.BlockSpec((tm, tn), lambda j, i: (i, j)),
        scratch_shapes=[pltpu.VMEM((k, tn), BF16)],
        compiler_params=_params(("parallel", "arbitrary")),
        name="matmul",
    )(a.reshape(m // dil, dil * k), b)


def _pack_halves(v):
    n = v.shape[1] // 2
    hi = lax.bitcast_convert_type(v[:, :n].astype(BF16).astype(F32), U32)
    lo = lax.bitcast_convert_type(v[:, n:].astype(BF16).astype(F32), U32)
    return (hi & jnp.uint32(0xFFFF0000)) | (lo >> 16)


def _unpack_halves(p):
    return (lax.bitcast_convert_type(p & jnp.uint32(0xFFFF0000), F32),
            lax.bitcast_convert_type(p << 16, F32))


def _layer_norm(v, gain, bias):
    mu = jnp.mean(v, axis=-1, keepdims=True)
    vc = v - mu
    var = jnp.mean(vc * vc, axis=-1, keepdims=True)
    return vc * lax.rsqrt(var + LN_EPS) * gain + bias


_V_GATE, _V_GAIN, _V_BIAS, _V_SCALE_A, _V_SHIFT_A, _V_SCALE_B, _V_SHIFT_B = range(7)


def _prologue_kernel(x_ref, vec_ref, o_ref):
    h = x_ref[...] * (1.0 + vec_ref[_V_SCALE_A:_V_SCALE_A + 1, :]) \
        + vec_ref[_V_SHIFT_A:_V_SHIFT_A + 1, :]
    o_ref[...] = h.astype(BF16)


def _prologue(x2, vecs, tm=256):
    s, d = x2.shape
    return pl.pallas_call(
        _prologue_kernel,
        out_shape=jax.ShapeDtypeStruct((s, d), BF16),
        grid=(s // tm,),
        in_specs=[pl.BlockSpec((tm, d), lambda i: (i, 0)),
                  pl.BlockSpec((8, d), lambda i: (0, 0))],
        out_specs=pl.BlockSpec((tm, d), lambda i: (i, 0)),
        compiler_params=_params(("parallel",)),
        name="prologue",
    )(x2, vecs)


def _ln_router_kernel(alpha, n_groups, n_per_group,
                      x_ref, mix_ref, vec_ref, wr_ref, br_ref, xo_ref, route_ref, hfp_ref):
    gate = vec_ref[_V_GATE:_V_GATE + 1, :]
    x1 = alpha * x_ref[...] + (1.0 + gate) * mix_ref[...].astype(F32)
    xn = _layer_norm(x1, vec_ref[_V_GAIN:_V_GAIN + 1, :], vec_ref[_V_BIAS:_V_BIAS + 1, :])
    xo_ref[...] = xn
    hf = xn * (1.0 + vec_ref[_V_SCALE_A:_V_SCALE_A + 1, :]) + vec_ref[_V_SHIFT_A:_V_SHIFT_A + 1, :]
    hfp_ref[...] = _pack_halves(hf)
    logits = jnp.dot(hf, wr_ref[...], preferred_element_type=F32, precision=HIGHEST) + br_ref[...]
    lane = lax.broadcasted_iota(jnp.int32, logits.shape, 1).astype(F32)
    ninf = -jnp.inf
    big = float(LANES)
    is_g = lane < n_groups
    gl = jnp.where(is_g, logits, ninf)
    gmax = jnp.max(gl, axis=-1, keepdims=True)
    gsel = jnp.min(jnp.where(gl == gmax, lane, big), axis=-1, keepdims=True)
    p_group = 1.0 / jnp.sum(jnp.where(is_g, jnp.exp(logits - gmax), 0.0), axis=-1, keepdims=True)
    lo = n_groups + gsel * n_per_group
    el = jnp.where((lane >= lo) & (lane < lo + n_per_group), logits, ninf)
    m1 = jnp.max(el, axis=-1, keepdims=True)
    i1 = jnp.min(jnp.where(el == m1, lane, big), axis=-1, keepdims=True)
    el2 = jnp.where(lane == i1, ninf, el)
    m2 = jnp.max(el2, axis=-1, keepdims=True)
    i2 = jnp.min(jnp.where(el2 == m2, lane, big), axis=-1, keepdims=True)
    e21 = jnp.exp(m2 - m1)
    inv = 1.0 / (1.0 + e21)
    w1 = p_group * inv
    w2 = p_group * (e21 * inv)
    route_ref[...] = jnp.where(lane == 0.0, i1 - n_groups,
                     jnp.where(lane == 1.0, i2 - n_groups,
                     jnp.where(lane == 2.0, w1,
                     jnp.where(lane == 3.0, w2, 0.0))))


def _ln_router(x2, mix, vecs, wr, br, alpha, n_groups, n_per_group, tm=256):
    s, d = x2.shape
    return pl.pallas_call(
        functools.partial(_ln_router_kernel, alpha, n_groups, n_per_group),
        out_shape=[jax.ShapeDtypeStruct((s, d), F32),
                   jax.ShapeDtypeStruct((s, LANES), F32),
                   jax.ShapeDtypeStruct((s, d // 2), U32)],
        grid=(s // tm,),
        in_specs=[pl.BlockSpec((tm, d), lambda i: (i, 0)),
                  pl.BlockSpec((tm, d), lambda i: (i, 0)),
                  pl.BlockSpec((8, d), lambda i: (0, 0)),
                  pl.BlockSpec((d, LANES), lambda i: (0, 0)),
                  pl.BlockSpec((1, LANES), lambda i: (0, 0))],
        out_specs=[pl.BlockSpec((tm, d), lambda i: (i, 0)),
                   pl.BlockSpec((tm, LANES), lambda i: (i, 0)),
                   pl.BlockSpec((tm, d // 2), lambda i: (i, 0))],
        compiler_params=_params(("parallel",)),
        name="ln_router",
    )(x2, mix, vecs, wr, br)


def _ln_combine_kernel(alpha, with_next, with_kv, x_ref, y0_ref, y1_ref, route_ref, vec_ref, *refs):
    w1, w2 = route_ref[:, 2:3], route_ref[:, 3:4]
    a0, b0 = _unpack_halves(y0_ref[...])
    a1, b1 = _unpack_halves(y1_ref[...])
    ffn = jnp.concatenate([a0 * w1 + a1 * w2, b0 * w1 + b1 * w2], axis=1)
    gate = vec_ref[_V_GATE:_V_GATE + 1, :]
    x1 = alpha * x_ref[...] + (1.0 + gate) * ffn
    xn = _layer_norm(x1, vec_ref[_V_GAIN:_V_GAIN + 1, :], vec_ref[_V_BIAS:_V_BIAS + 1, :])
    refs[0][...] = xn
    k = 1
    if with_kv:
        hk = xn * (1.0 + vec_ref[_V_SCALE_B:_V_SCALE_B + 1, :]) + vec_ref[_V_SHIFT_B:_V_SHIFT_B + 1, :]
        refs[k][...] = hk.astype(BF16)
        k += 1
    if with_next:
        h = xn * (1.0 + vec_ref[_V_SCALE_A:_V_SCALE_A + 1, :]) + vec_ref[_V_SHIFT_A:_V_SHIFT_A + 1, :]
        refs[k][...] = h.astype(BF16)


def _ln_combine(x2, y2, route, vecs, alpha, with_next, with_kv, tm=256):
    s, d = x2.shape
    row = lambda i: (i, 0)
    n_bf = int(with_kv) + int(with_next)
    shapes = [jax.ShapeDtypeStruct((s, d), F32)] + [jax.ShapeDtypeStruct((s, d), BF16)] * n_bf
    specs = [pl.BlockSpec((tm, d), row)] * (1 + n_bf)
    return pl.pallas_call(
        functools.partial(_ln_combine_kernel, alpha, with_next, with_kv),
        out_shape=shapes,
        grid=(s // tm,),
        in_specs=[pl.BlockSpec((tm, d), row),
                  pl.BlockSpec((tm, d // 2), row),
                  pl.BlockSpec((tm, d // 2), lambda i: (i + s // tm, 0)),
                  pl.BlockSpec((tm, LANES), row),
                  pl.BlockSpec((8, d), lambda i: (0, 0))],
        out_specs=specs,
        compiler_params=_params(("parallel",)),
        name="ln_combine",
    )(x2, y2, y2, route, vecs)


def _dilated_kernel(dil, blocks_per_stream, slopes, q_ref, kp_ref, kc_ref, vp_ref, vc_ref,
                    o_ref, lse_ref):
    nb = pl.program_id(0) % blocks_per_stream
    has_prev = nb > 0
    qi = lax.broadcasted_iota(jnp.int32, (A_BLOCK, A_BLOCK), 0)
    kj = lax.broadcasted_iota(jnp.int32, (A_BLOCK, A_BLOCK), 1)
    dist_prev = jnp.where((kj >= qi) & has_prev, ((qi + A_BLOCK - kj) * dil).astype(F32), -NEG)
    dist_cur = jnp.where(kj <= qi, ((qi - kj) * dil).astype(F32), -NEG)
    sqrt_e = math.sqrt(HEAD_DIM)
    lane = lax.broadcasted_iota(jnp.int32, (A_BLOCK, LANES), 1)
    ones = jnp.ones((A_BLOCK, LANES), BF16)
    n_heads = len(slopes)

    def scores(h):
        hs = slice(h * HEAD_DIM, (h + 1) * HEAD_DIM)
        q = q_ref[:, hs]
        return (lax.dot_general(q, kp_ref[:, hs], _NT, preferred_element_type=F32),
                lax.dot_general(q, kc_ref[:, hs], _NT, preferred_element_type=F32))

    ahead = 6
    pending = {h: scores(h) for h in range(min(ahead, n_heads))}
    lse_all = jnp.zeros((A_BLOCK, LANES), F32)
    for h, slope in enumerate(slopes):
        hs = slice(h * HEAD_DIM, (h + 1) * HEAD_DIM)
        s_p, s_c = pending.pop(h)
        t_p = s_p - (slope * sqrt_e) * dist_prev
        t_c = s_c - (slope * sqrt_e) * dist_cur
        t_max = jnp.max(jnp.maximum(t_p, t_c), axis=-1, keepdims=True)
        p_p = jnp.exp2((t_p - t_max) * (LOG2E / sqrt_e)).astype(BF16)
        p_c = jnp.exp2((t_c - t_max) * (LOG2E / sqrt_e)).astype(BF16)
        if h + ahead < n_heads:
            pending[h + ahead] = scores(h + ahead)
        o = jnp.dot(p_p, vp_ref[:, hs], preferred_element_type=F32) \
            + jnp.dot(p_c, vc_ref[:, hs], preferred_element_type=F32)
        l = jnp.dot(p_p, ones, preferred_element_type=F32) \
            + jnp.dot(p_c, ones, preferred_element_type=F32)
        o_ref[:, hs] = o * (1.0 / l)
        lse_all = jnp.where(lane == h, t_max * (1.0 / sqrt_e) + jnp.log(l), lse_all)
    lse_ref[...] = lse_all


def _dilated_attention(qkv, dil, n_heads):
    s = qkv.shape[0]
    w = n_heads * HEAD_DIM
    n = s // dil
    bps = n // A_BLOCK
    slopes = [float(v) for v in np.exp2(-8.0 * np.arange(1, n_heads + 1, dtype=np.float32) / n_heads)]
    prev = lambda b: (jnp.maximum(b - 1, 0), 1)
    o, lse = pl.pallas_call(
        functools.partial(_dilated_kernel, dil, bps, slopes),
        out_shape=[jax.ShapeDtypeStruct((n, dil * w), F32),
                   jax.ShapeDtypeStruct((n, dil * LANES), F32)],
        grid=(s // A_BLOCK,),
        in_specs=[pl.BlockSpec((A_BLOCK, w), lambda b: (b, 0)),
                  pl.BlockSpec((A_BLOCK, w), prev),
                  pl.BlockSpec((A_BLOCK, w), lambda b: (b, 1)),
                  pl.BlockSpec((A_BLOCK, w), lambda b: (jnp.maximum(b - 1, 0), 2)),
                  pl.BlockSpec((A_BLOCK, w), lambda b: (b, 2))],
        out_specs=[pl.BlockSpec((A_BLOCK, w), lambda b: (b % bps, b // bps)),
                   pl.BlockSpec((A_BLOCK, LANES), lambda b: (b % bps, b // bps))],
        compiler_params=_params(("parallel",)),
        name="dilated_attention",
    )(qkv, qkv, qkv, qkv, qkv)
    return o, lse


def _merge_kernel(n_heads, *refs):
    n_g = len(A_GROUPS)
    o_refs, l_refs = refs[:n_g], refs[n_g:2 * n_g]
    out_ref = refs[2 * n_g]
    stages = iter(refs[2 * n_g + 1:])
    tm = out_ref.shape[0]
    o_nat, l_nat = [], []
    for (_, dil), o_ref, l_ref in zip(A_GROUPS, o_refs, l_refs):
        if dil == 1:
            o_nat.append(lambda h, o_ref=o_ref: o_ref[:, h * HEAD_DIM:(h + 1) * HEAD_DIM])
            l_nat.append(l_ref[...])
            continue
        stage = next(stages)
        rows = tm // dil
        w = o_ref.shape[1] // dil
        for r in range(dil):
            for h in range(n_heads):
                stage[h, pl.ds(r, rows, stride=dil), :] = \
                    o_ref[:, r * w + h * HEAD_DIM:r * w + (h + 1) * HEAD_DIM]
            stage[n_heads, pl.ds(r, rows, stride=dil), :] = l_ref[:, r * LANES:(r + 1) * LANES]
        o_nat.append(lambda h, stage=stage: stage[h])
        l_nat.append(stage[n_heads])
    m = functools.reduce(jnp.maximum, l_nat)
    es = [jnp.exp(l - m) for l in l_nat]
    inv = 1.0 / functools.reduce(lambda a, b: a + b, es)
    ws = [e * inv for e in es]
    for h in range(n_heads):
        acc = functools.reduce(lambda a, b: a + b,
                               [o(h) * w_g[:, h:h + 1] for o, w_g in zip(o_nat, ws)])
        out_ref[:, h * HEAD_DIM:(h + 1) * HEAD_DIM] = acc.astype(BF16)


def _merge_groups(os_, lses, n_heads, tm=256):
    w = n_heads * HEAD_DIM
    s = os_[0].shape[0] * A_GROUPS[0][1]
    row = lambda i: (i, 0)
    dils = [dil for _, dil in A_GROUPS]
    return pl.pallas_call(
        functools.partial(_merge_kernel, n_heads),
        out_shape=jax.ShapeDtypeStruct((s, w), BF16),
        grid=(s // tm,),
        in_specs=[pl.BlockSpec((tm // dil, dil * w), row) for dil in dils]
        + [pl.BlockSpec((tm // dil, dil * LANES), row) for dil in dils],
        out_specs=pl.BlockSpec((tm, w), row),
        scratch_shapes=[pltpu.VMEM((n_heads + 1, tm, LANES), F32) for dil in dils if dil > 1],
        compiler_params=_params(("parallel",)),
        name="merge_groups",
    )(*os_, *lses)


def _kmean_kernel(k_ref, o_ref):
    o_ref[...] = jnp.mean(k_ref[...], axis=0, keepdims=True)


def _block_means(k):
    s, kw = k.shape
    nb = s // MOBA_BLOCK
    return pl.pallas_call(
        _kmean_kernel,
        out_shape=jax.ShapeDtypeStruct((nb, 1, kw), F32),
        grid=(nb,),
        in_specs=[pl.BlockSpec((MOBA_BLOCK, kw), lambda i: (i, 0))],
        out_specs=pl.BlockSpec((None, 1, kw), lambda i: (i, 0, 0)),
        compiler_params=_params(("parallel",)),
        name="block_means",
    )(k)


def _moba_kernel(q_ref, k_ref, vt_ref, kmean_ref, slope_ref, o_ref,
                 qs_ref, sel_ref, bias_ref, m_ref, l_ref, acc_ref):
    qb = pl.program_id(1)
    n_blk = kmean_ref.shape[0]
    rows = MOBA_REP * MOBA_BLOCK
    n_chunks = rows // LANES
    slope2 = slope_ref[...] * LOG2E

    @pl.when(qb == 0)
    def _():
        lane_t = (lax.broadcasted_iota(jnp.int32, (1, rows), 1) % MOBA_BLOCK).astype(F32)
        sub = lax.broadcasted_iota(jnp.int32, (MOBA_BLOCK, 1), 0).astype(F32)
        bias_ref[...] = slope2 * (lane_t - sub)

    q = q_ref[...]
    q4 = jnp.concatenate([q[:, r * HEAD_DIM:(r + 1) * HEAD_DIM] for r in range(MOBA_REP)], axis=0)
    gate = lax.dot_general(kmean_ref[...], q4, _NT, preferred_element_type=F32, precision=HIGHEST)
    blk = lax.broadcasted_iota(jnp.int32, (n_blk, rows), 0).astype(F32)
    ninf = -jnp.inf
    gate = jnp.where(blk < qb.astype(F32), gate, ninf)
    sel = jnp.zeros((n_blk, rows), F32)
    for _ in range(MOBA_TOPK):
        top = jnp.max(gate, axis=0, keepdims=True)
        idx = jnp.min(jnp.where(gate == top, blk, float(n_blk)), axis=0, keepdims=True)
        hit = (blk == idx) & (top > ninf)
        sel = jnp.where(hit, 1.0, sel)
        gate = jnp.where(hit, ninf, gate)
    sel_ref[...] = sel
    qs_ref[...] = (q4 * (LOG2E / math.sqrt(HEAD_DIM))).astype(BF16)

    kk = lax.broadcasted_iota(jnp.int32, (MOBA_BLOCK, LANES), 0)
    tt = lax.broadcasted_iota(jnp.int32, (MOBA_BLOCK, LANES), 1)

    def run_blocks(blocks, own):
        k_b = [k_ref[n] for n in blocks]
        vt_b = [vt_ref[n] for n in blocks]
        sel_b = [None if own else sel_ref[pl.ds(n, 1), :] for n in blocks]
        bt_b = [None if own else slope2 * ((n - qb) * MOBA_BLOCK).astype(F32) for n in blocks]
        tasks = [(bi, c) for bi in range(len(blocks)) for c in range(n_chunks)]

        def score(t):
            bi, c = tasks[t]
            return lax.dot_general(k_b[bi], qs_ref[c * LANES:(c + 1) * LANES, :], _NT,
                                   preferred_element_type=F32)

        ahead = 2
        pending = {t: score(t) for t in range(min(ahead, len(tasks)))}
        for t, (bi, c) in enumerate(tasks):
            cs = slice(c * LANES, (c + 1) * LANES)
            u = pending.pop(t) - bias_ref[:, cs]
            if own:
                u = jnp.where(kk <= tt + (c * LANES) % MOBA_BLOCK, u, NEG)
                m_new = jnp.max(u, axis=0, keepdims=True)
                p = jnp.exp2(u - m_new)
                l_new = jnp.sum(p, axis=0, keepdims=True)
            else:
                picked = sel_b[bi][:, cs] > 0.0
                bt = bt_b[bi][:, cs]
                m_old = m_ref[:, cs]
                m_new = jnp.maximum(
                    m_old, jnp.where(picked, jnp.max(u, axis=0, keepdims=True) + bt, NEG))
                p = jnp.exp2(u - jnp.where(picked, m_new - bt, -NEG))
                alpha = jnp.exp2(m_old - m_new)
                l_new = alpha * l_ref[:, cs] + jnp.sum(p, axis=0, keepdims=True)
            m_ref[:, cs] = m_new
            l_ref[:, cs] = l_new
            if t + ahead < len(tasks):
                pending[t + ahead] = score(t + ahead)
            pv = jnp.dot(vt_b[bi], p.astype(BF16), preferred_element_type=F32)
            acc_ref[:, cs] = pv if own else alpha * acc_ref[:, cs] + pv

    run_blocks([qb], own=True)

    def body(i, carry):
        run_blocks([i * MOBA_UNROLL + j for j in range(MOBA_UNROLL)], own=False)
        return carry

    lax.fori_loop(0, (qb + MOBA_UNROLL - 1) // MOBA_UNROLL, body, 0)
    o_t = acc_ref[...] * (1.0 / l_ref[...])
    for r in range(MOBA_REP):
        o_ref[:, r * HEAD_DIM:(r + 1) * HEAD_DIM] = \
            o_t[:, r * MOBA_BLOCK:(r + 1) * MOBA_BLOCK].T.astype(BF16)


def _moba_attention(q, k_blocks, vt_blocks, k_mean, slope_lanes):
    s = q.shape[0]
    g, nb = k_blocks.shape[:2]
    assert nb % MOBA_UNROLL == 0
    rows = MOBA_REP * MOBA_BLOCK
    qw = MOBA_REP * HEAD_DIM
    return pl.pallas_call(
        _moba_kernel,
        out_shape=jax.ShapeDtypeStruct((s, g * qw), BF16),
        grid=(g, nb),
        in_specs=[pl.BlockSpec((MOBA_BLOCK, qw), lambda gi, qb: (qb, gi)),
                  pl.BlockSpec((None, nb, MOBA_BLOCK, HEAD_DIM), lambda gi, qb: (gi, 0, 0, 0)),
                  pl.BlockSpec((None, nb, HEAD_DIM, MOBA_BLOCK), lambda gi, qb: (gi, 0, 0, 0)),
                  pl.BlockSpec((None, nb, HEAD_DIM), lambda gi, qb: (gi, 0, 0)),
                  pl.BlockSpec((None, 1, rows), lambda gi, qb: (gi, 0, 0))],
        out_specs=pl.BlockSpec((MOBA_BLOCK, qw), lambda gi, qb: (qb, gi)),
        scratch_shapes=[pltpu.VMEM((rows, HEAD_DIM), BF16),
                        pltpu.VMEM((nb, rows), F32),
                        pltpu.VMEM((MOBA_BLOCK, rows), F32),
                        pltpu.VMEM((1, rows), F32),
                        pltpu.VMEM((1, rows), F32),
                        pltpu.VMEM((HEAD_DIM, rows), F32)],
        compiler_params=_params(("parallel", "arbitrary")),
        name="moba_attention",
    )(q, k_blocks, vt_blocks, k_mean, slope_lanes)


def _moe_kernel(blk_e_ref, total_ref, blk_first_ref, slot_tok_ref, slot_dst_ref,
                x_hbm, wg_ref, wu_ref, wd_ref, y_hbm,
                xbuf, obuf, wg_bf, wu_bf, wd_bf, gsem, ssem):
    b = pl.program_id(0)
    total = total_ref[0]
    slot = b % 2

    def gather_copy(blk, j, buf_slot):
        tok = slot_tok_ref[blk * MOE_BLOCK + j]
        return pltpu.make_async_copy(x_hbm.at[pl.ds(tok, 1)], xbuf.at[buf_slot, pl.ds(j, 1)],
                                     gsem.at[buf_slot])

    def scatter_copy(blk, j, buf_slot):
        dst = slot_dst_ref[blk * MOE_BLOCK + j]
        return pltpu.make_async_copy(obuf.at[buf_slot, pl.ds(j, 1)], y_hbm.at[pl.ds(dst, 1)],
                                     ssem.at[buf_slot])

    def wait_gather(buf_slot):
        pltpu.make_async_copy(x_hbm.at[pl.ds(0, MOE_BLOCK)], xbuf.at[buf_slot],
                              gsem.at[buf_slot]).wait()

    def wait_scatter(buf_slot):
        pltpu.make_async_copy(obuf.at[buf_slot], y_hbm.at[pl.ds(0, MOE_BLOCK)],
                              ssem.at[buf_slot]).wait()

    @pl.when(b == 0)
    def _():
        n_real = y_hbm.shape[0] - 2 * MOE_BLOCK
        obuf[0] = jnp.zeros(obuf.shape[1:], U32)
        for i in range(2):
            dump = y_hbm.at[pl.ds(n_real + i * MOE_BLOCK, MOE_BLOCK)]
            pltpu.make_async_copy(obuf.at[0], dump, ssem.at[i]).start()
        for i in range(2):
            wait_scatter(i)
        for j in range(MOE_BLOCK):
            gather_copy(0, j, 0).start()

    @pl.when(b < total)
    def _():
        @pl.when(blk_first_ref[b] == 1)
        def _():
            wg_bf[...] = wg_ref[...].astype(BF16)
            wu_bf[...] = wu_ref[...].astype(BF16)
            wd_bf[...] = wd_ref[...].astype(BF16)

        wait_gather(slot)
        for j in range(MOE_BLOCK):
            gather_copy(b + 1, j, 1 - slot).start(priority=j % 2)
        h_lo, h_hi = _unpack_halves(xbuf[slot])
        hb = jnp.concatenate([h_lo.astype(BF16), h_hi.astype(BF16)], axis=1)
        gt = jnp.dot(hb, wg_bf[...], preferred_element_type=F32)
        up = jnp.dot(hb, wu_bf[...], preferred_element_type=F32)
        hid = gt * (1.0 / (1.0 + jnp.exp(-gt))) * up
        obuf[slot] = _pack_halves(jnp.dot(hid.astype(BF16), wd_bf[...],
                                          preferred_element_type=F32))
        for j in range(MOE_BLOCK):
            scatter_copy(b, j, slot).start(priority=j % 2)

        @pl.when(b >= 1)
        def _():
            wait_scatter(1 - slot)

        @pl.when(b == total - 1)
        def _():
            wait_scatter(slot)
            wait_gather(1 - slot)


def _moe_dispatch(ids, n_experts):
    s, k = ids.shape
    a = s * k
    n_blocks = -(-a // MOE_BLOCK) + n_experts
    e_flat = ids.reshape(a)
    order = jnp.argsort(e_flat, stable=True).astype(jnp.int32)
    experts = jnp.arange(n_experts, dtype=jnp.int32)
    counts = jnp.sum((e_flat[:, None] == experts[None, :]).astype(jnp.int32), axis=0)
    starts = jnp.cumsum(counts) - counts
    nblk = (counts + MOE_BLOCK - 1) // MOE_BLOCK
    blk_end = jnp.cumsum(nblk)
    blk_start = blk_end - nblk
    total = blk_end[-1]
    bidx = jnp.arange(n_blocks, dtype=jnp.int32)
    used = bidx < total
    b_eff = jnp.minimum(bidx, total - 1)
    blk_e = jnp.sum((blk_end[None, :] <= b_eff[:, None]).astype(jnp.int32), axis=1)
    blk_e = jnp.minimum(blk_e, n_experts - 1)
    rank0 = (bidx - blk_start[blk_e]) * MOE_BLOCK
    blk_first = (used & (rank0 == 0)).astype(jnp.int32)
    lane = jnp.arange(MOE_BLOCK, dtype=jnp.int32)[None, :]
    rank = rank0[:, None] + lane
    valid = used[:, None] & (rank < counts[blk_e][:, None])
    src = jnp.clip(starts[blk_e][:, None] + rank, 0, a - 1)
    asg = order[src]
    slot_tok = jnp.where(valid, asg // k, 0)
    slot_tok = jnp.concatenate([slot_tok, jnp.zeros((1, MOE_BLOCK), jnp.int32)]).reshape(-1)
    dump = a + (bidx[:, None] % 2) * MOE_BLOCK + lane
    slot_dst = jnp.where(valid, (asg % k) * s + asg // k, dump).reshape(-1)
    return blk_e, total.reshape(1), blk_first, slot_tok, slot_dst, n_blocks


def _moe_experts(hfp, ids, layer, w_gate, w_up, w_down):
    s, dp = hfp.shape
    d = 2 * dp
    n_experts, _, f = w_gate.shape[1:]
    k = ids.shape[1]
    blk_e, total, blk_first, slot_tok, slot_dst, n_blocks = _moe_dispatch(ids, n_experts)
    w_idx = lambda b, be, *_: (layer, be[b], 0, 0)
    return pl.pallas_call(
        _moe_kernel,
        out_shape=jax.ShapeDtypeStruct((s * k + 2 * MOE_BLOCK, dp), U32),
        grid_spec=pltpu.PrefetchScalarGridSpec(
            num_scalar_prefetch=5,
            grid=(n_blocks,),
            in_specs=[pl.BlockSpec(memory_space=pl.ANY),
                      pl.BlockSpec((None, None, d, f), w_idx),
                      pl.BlockSpec((None, None, d, f), w_idx),
                      pl.BlockSpec((None, None, f, d), w_idx)],
            out_specs=pl.BlockSpec(memory_space=pl.ANY),
            scratch_shapes=[pltpu.VMEM((2, MOE_BLOCK, dp), U32),
                            pltpu.VMEM((2, MOE_BLOCK, dp), U32),
                            pltpu.VMEM((d, f), BF16),
                            pltpu.VMEM((d, f), BF16),
                            pltpu.VMEM((f, d), BF16),
                            pltpu.SemaphoreType.DMA((2,)),
                            pltpu.SemaphoreType.DMA((2,))]),
        compiler_params=_params(("arbitrary",)),
        name="moe_experts",
    )(blk_e, total, blk_first, slot_tok, slot_dst, hfp, w_gate, w_up, w_down)


def _vec_pack(d, **rows):
    out = jnp.zeros((8, d), F32)
    for name, v in rows.items():
        out = out.at[name_to_row[name]].set(v)
    return out


name_to_row = {"gate": _V_GATE, "gain": _V_GAIN, "bias": _V_BIAS, "scale_a": _V_SCALE_A,
               "shift_a": _V_SHIFT_A, "scale_b": _V_SCALE_B, "shift_b": _V_SHIFT_B}


def kernel(x, c, ada_down, ada_up, ada_bias, ln_gain, ln_bias, a_w_qkv, a_w_o, kv_ada_down, kv_ada_up, kv_ada_bias, kv_w, b_w_q, b_w_o, moe_w_group, moe_b_group, moe_w_expert, moe_b_expert, moe_w_gate, moe_w_up, moe_w_down):
    batch, s, d = x.shape
    assert batch == 1
    depth = ada_down.shape[0]
    n_a = a_w_qkv.shape[0]
    alpha = (2 * depth) ** 0.25
    a_heads = a_w_o.shape[1] // HEAD_DIM
    a_width = a_heads * HEAD_DIM
    n_groups = moe_w_group.shape[-1]
    n_per_group = moe_w_expert.shape[-1]
    kv_heads = kv_w.shape[1] // (2 * HEAD_DIM)
    kv_width = kv_heads * HEAD_DIM
    b_heads = kv_heads * MOBA_REP
    n_blk = s // MOBA_BLOCK

    mods = _modulation(c, ada_down, ada_up, ada_bias).reshape(depth, 6, d)
    kv_mod = _modulation(c, kv_ada_down[None], kv_ada_up[None], kv_ada_bias[None]).reshape(2, d)

    x2 = x.reshape(s, d)
    hs = _prologue(x2, _vec_pack(d, scale_a=mods[0, 1], shift_a=mods[0, 0]))

    k_blocks = vt_blocks = k_mean = slope_lanes = None
    for l in range(depth):
        shift1, scale1, gate1, shift2, scale2, gate2 = [mods[l, i] for i in range(6)]
        if l < n_a:
            outs, lses = [], []
            for g, (_, dil) in enumerate(A_GROUPS):
                qkv = _matmul(hs, a_w_qkv, l, BF16, n_off=g * 3 * a_width, n_out=3 * a_width,
                              dil=dil)
                o_g, lse_g = _dilated_attention(qkv, dil, a_heads)
                outs.append(o_g)
                lses.append(lse_g)
            merged = _merge_groups(outs, lses, a_heads)
            mix = _matmul(merged, a_w_o, l, BF16)
        else:
            j = l - n_a
            q = _matmul(hs, b_w_q, j, F32)
            o = _moba_attention(q, k_blocks, vt_blocks, k_mean, slope_lanes)
            mix = _matmul(o, b_w_o, j, BF16)

        wr = jnp.concatenate(
            [moe_w_group[l], jnp.moveaxis(moe_w_expert[l], 0, 1).reshape(d, n_groups * n_per_group),
             jnp.zeros((d, LANES - n_groups * (1 + n_per_group)), F32)], axis=1)
        br = jnp.concatenate(
            [moe_b_group[l], moe_b_expert[l].reshape(-1),
             jnp.zeros((LANES - n_groups * (1 + n_per_group),), F32)]).reshape(1, LANES)
        vec1 = _vec_pack(d, gate=gate1, gain=ln_gain[l, 0], bias=ln_bias[l, 0],
                         scale_a=scale2, shift_a=shift2)
        x2, route, hfp = _ln_router(x2, mix, vec1, wr, br, alpha, n_groups, n_per_group)
        ids = route[:, :MOE_TOPK].astype(jnp.int32)
        y2 = _moe_experts(hfp, ids, l, moe_w_gate, moe_w_up, moe_w_down)

        last = l == depth - 1
        with_kv = l == n_a - 1 and n_a < depth
        rows = dict(gate=gate2, gain=ln_gain[l, 1], bias=ln_bias[l, 1])
        if not last:
            rows.update(scale_a=mods[l + 1, 1], shift_a=mods[l + 1, 0])
        if with_kv:
            rows.update(scale_b=kv_mod[1], shift_b=kv_mod[0])
        res = _ln_combine(x2, y2, route, _vec_pack(d, **rows), alpha, not last, with_kv)
        x2 = res[0]
        if with_kv:
            kv = _matmul(res[1], kv_w[None], 0, F32)
            k_f32 = kv[:, :kv_width]
            k_mean = _block_means(k_f32).reshape(n_blk, kv_heads, HEAD_DIM).transpose(1, 0, 2)
            k_blocks = k_f32.astype(BF16).reshape(n_blk, MOBA_BLOCK, kv_heads, HEAD_DIM) \
                .transpose(2, 0, 1, 3)
            vt_blocks = kv[:, kv_width:].astype(BF16).reshape(n_blk, MOBA_BLOCK, kv_heads, HEAD_DIM) \
                .transpose(2, 0, 3, 1)
            slopes = jnp.exp2(-8.0 * jnp.arange(1, b_heads + 1, dtype=F32) / b_heads)
            slope_lanes = jnp.repeat(slopes.reshape(kv_heads, MOBA_REP), MOBA_BLOCK, axis=1) \
                .reshape(kv_heads, 1, MOBA_REP * MOBA_BLOCK)
        if not last:
            hs = res[-1]
    return x2.reshape(batch, s, d)
```

```python
import functools
import math

import numpy as np
import jax
import jax.numpy as jnp
from jax import lax
from jax.experimental import pallas as pl
from jax.experimental.pallas import tpu as pltpu

F32 = jnp.float32
BF16 = jnp.bfloat16
U32 = jnp.uint32
HIGHEST = lax.Precision.HIGHEST

HEAD_DIM = 128
A_GROUPS = ((128, 1), (512, 4), (2048, 16))
A_BLOCK = 128
MOBA_BLOCK = 256
MOBA_TOPK = 3
MOBA_REP = 4
MOBA_UNROLL = 4
MOE_TOPK = 2
MOE_BLOCK = 128
MOE_CHUNKS = 4
LN_EPS = 1e-5
NEG = -1e30
LOG2E = math.log2(math.e)
LANES = 128
VMEM_LIMIT = 56 * 1024 * 1024

_NT = (((1,), (1,)), ((), ()))


def _params(sem, vmem=VMEM_LIMIT):
    return pltpu.CompilerParams(dimension_semantics=sem, vmem_limit_bytes=vmem)


def _mod_kernel(c_ref, wd_ref, wu_ref, b_ref, o_ref, t_ref):
    @pl.when(pl.program_id(1) == 0)
    def _():
        c = c_ref[...]
        sc = c * (1.0 / (1.0 + jnp.exp(-c)))
        t_ref[...] = jnp.dot(sc, wd_ref[...], preferred_element_type=F32, precision=HIGHEST)

    o_ref[...] = jnp.dot(t_ref[...], wu_ref[...], preferred_element_type=F32,
                         precision=HIGHEST) + b_ref[...]


def _modulation(c, w_down, w_up, b_up):
    n_l, d, r = w_down.shape
    n = w_up.shape[-1]
    tn = n // pl.cdiv(n, 6144)
    c8 = jnp.broadcast_to(c, (8, d))
    out = pl.pallas_call(
        _mod_kernel,
        out_shape=jax.ShapeDtypeStruct((n_l, 8, n), F32),
        grid=(n_l, n // tn),
        in_specs=[
            pl.BlockSpec((8, d), lambda l, j: (0, 0)),
            pl.BlockSpec((None, d, r), lambda l, j: (l, 0, 0)),
            pl.BlockSpec((None, r, tn), lambda l, j: (l, 0, j)),
            pl.BlockSpec((None, 1, tn), lambda l, j: (l, 0, j)),
        ],
        out_specs=pl.BlockSpec((None, 8, tn), lambda l, j: (l, 0, j)),
        scratch_shapes=[pltpu.VMEM((8, r), F32)],
        compiler_params=_params(("arbitrary", "arbitrary")),
        name="modulation",
    )(c8, w_down, w_up, b_up.reshape(n_l, 1, n))
    return out[:, 0, :]


def _mm_kernel(a_ref, b_ref, o_ref, b_bf_ref):
    @pl.when(pl.program_id(1) == 0)
    def _():
        b_bf_ref[...] = b_ref[...].astype(BF16)

    o_ref[...] = jnp.dot(a_ref[...], b_bf_ref[...],
                         preferred_element_type=F32).astype(o_ref.dtype)


def _matmul(a, b, layer, out_dtype, n_off=0, n_out=None, dil=1, tm=1024, tn=512):
    m, k = a.shape
    n_out = b.shape[2] if n_out is None else n_out
    tm = min(tm, m // dil)
    tn = min(tn, n_out)
    off = n_off // tn
    bps = m // dil // tm
    return pl.pallas_call(
        _mm_kernel,
        out_shape=jax.ShapeDtypeStruct((m, n_out), out_dtype),
        grid=(n_out // tn, m // tm),
        in_specs=[
            pl.BlockSpec((tm, k), lambda j, i: (i % bps, i // bps)),
            pl.BlockSpec((None, k, tn), lambda j, i: (layer, 0, j + off)),
        ],
        out_specs=pl.BlockSpec((tm, tn), lambda j, i: (i, j)),
        scratch_shapes=[pltpu.VMEM((k, tn), BF16)],
        compiler_params=_params(("parallel", "arbitrary")),
        name="matmul",
    )(a.reshape(m // dil, dil * k), b)


def _pack_halves(v):
    n = v.shape[1] // 2
    hi = lax.bitcast_convert_type(v[:, :n].astype(BF16).astype(F32), U32)
    lo = lax.bitcast_convert_type(v[:, n:].astype(BF16).astype(F32), U32)
    return (hi & jnp.uint32(0xFFFF0000)) | (lo >> 16)


def _unpack_halves(p):
    return (lax.bitcast_convert_type(p & jnp.uint32(0xFFFF0000), F32),
            lax.bitcast_convert_type(p << 16, F32))


def _layer_norm(v, gain, bias):
    mu = jnp.mean(v, axis=-1, keepdims=True)
    vc = v - mu
    var = jnp.mean(vc * vc, axis=-1, keepdims=True)
    return vc * lax.rsqrt(var + LN_EPS) * gain + bias


_V_GATE, _V_GAIN, _V_BIAS, _V_SCALE_A, _V_SHIFT_A, _V_SCALE_B, _V_SHIFT_B = range(7)


def _prologue_kernel(x_ref, vec_ref, o_ref):
    h = x_ref[...] * (1.0 + vec_ref[_V_SCALE_A:_V_SCALE_A + 1, :]) \
        + vec_ref[_V_SHIFT_A:_V_SHIFT_A + 1, :]
    o_ref[...] = h.astype(BF16)


def _prologue(x2, vecs, tm=256):
    s, d = x2.shape
    return pl.pallas_call(
        _prologue_kernel,
        out_shape=jax.ShapeDtypeStruct((s, d), BF16),
        grid=(s // tm,),
        in_specs=[pl.BlockSpec((tm, d), lambda i: (i, 0)),
                  pl.BlockSpec((8, d), lambda i: (0, 0))],
        out_specs=pl.BlockSpec((tm, d), lambda i: (i, 0)),
        compiler_params=_params(("parallel",)),
        name="prologue",
    )(x2, vecs)


def _ln_router_kernel(alpha, n_groups, n_per_group,
                      x_ref, mix_ref, vec_ref, wrh_ref, wrl_ref, br_ref, xo_ref, route_ref, hfp_ref):
    gate = vec_ref[_V_GATE:_V_GATE + 1, :]
    x1 = alpha * x_ref[...] + (1.0 + gate) * mix_ref[...].astype(F32)
    xn = _layer_norm(x1, vec_ref[_V_GAIN:_V_GAIN + 1, :], vec_ref[_V_BIAS:_V_BIAS + 1, :])
    xo_ref[...] = xn
    hf = xn * (1.0 + vec_ref[_V_SCALE_A:_V_SCALE_A + 1, :]) + vec_ref[_V_SHIFT_A:_V_SHIFT_A + 1, :]
    hfp_ref[...] = _pack_halves(hf)
    hf_hi = hf.astype(BF16)
    hf_lo = (hf - hf_hi.astype(F32)).astype(BF16)
    logits = jnp.dot(hf_hi, wrh_ref[...], preferred_element_type=F32) \
        + jnp.dot(hf_hi, wrl_ref[...], preferred_element_type=F32) \
        + jnp.dot(hf_lo, wrh_ref[...], preferred_element_type=F32) + br_ref[...]
    lane = lax.broadcasted_iota(jnp.int32, logits.shape, 1).astype(F32)
    ninf = -jnp.inf
    big = float(LANES)
    is_g = lane < n_groups
    gl = jnp.where(is_g, logits, ninf)
    gmax = jnp.max(gl, axis=-1, keepdims=True)
    gsel = jnp.min(jnp.where(gl == gmax, lane, big), axis=-1, keepdims=True)
    p_group = 1.0 / jnp.sum(jnp.where(is_g, jnp.exp(logits - gmax), 0.0), axis=-1, keepdims=True)
    lo = n_groups + gsel * n_per_group
    el = jnp.where((lane >= lo) & (lane < lo + n_per_group), logits, ninf)
    m1 = jnp.max(el, axis=-1, keepdims=True)
    i1 = jnp.min(jnp.where(el == m1, lane, big), axis=-1, keepdims=True)
    el2 = jnp.where(lane == i1, ninf, el)
    m2 = jnp.max(el2, axis=-1, keepdims=True)
    i2 = jnp.min(jnp.where(el2 == m2, lane, big), axis=-1, keepdims=True)
    e21 = jnp.exp(m2 - m1)
    inv = 1.0 / (1.0 + e21)
    w1 = p_group * inv
    w2 = p_group * (e21 * inv)
    route_ref[...] = jnp.where(lane == 0.0, i1 - n_groups,
                     jnp.where(lane == 1.0, i2 - n_groups,
                     jnp.where(lane == 2.0, w1,
                     jnp.where(lane == 3.0, w2, 0.0))))


def _ln_router(x2, mix, vecs, wr, br, alpha, n_groups, n_per_group, tm=256):
    s, d = x2.shape
    wr_top = lax.bitcast_convert_type(
        lax.bitcast_convert_type(wr, U32) & jnp.uint32(0xFFFF0000), F32)
    wr_hi = wr_top.astype(BF16)
    wr_lo = (wr - wr_top).astype(BF16)
    return pl.pallas_call(
        functools.partial(_ln_router_kernel, alpha, n_groups, n_per_group),
        out_shape=[jax.ShapeDtypeStruct((s, d), F32),
                   jax.ShapeDtypeStruct((s, LANES), F32),
                   jax.ShapeDtypeStruct((s, d // 2), U32)],
        grid=(s // tm,),
        in_specs=[pl.BlockSpec((tm, d), lambda i: (i, 0)),
                  pl.BlockSpec((tm, d), lambda i: (i, 0)),
                  pl.BlockSpec((8, d), lambda i: (0, 0)),
                  pl.BlockSpec((d, LANES), lambda i: (0, 0)),
                  pl.BlockSpec((d, LANES), lambda i: (0, 0)),
                  pl.BlockSpec((1, LANES), lambda i: (0, 0))],
        out_specs=[pl.BlockSpec((tm, d), lambda i: (i, 0)),
                   pl.BlockSpec((tm, LANES), lambda i: (i, 0)),
                   pl.BlockSpec((tm, d // 2), lambda i: (i, 0))],
        compiler_params=_params(("parallel",)),
        name="ln_router",
    )(x2, mix, vecs, wr_hi, wr_lo, br)


def _ln_combine_kernel(alpha, with_next, with_kv, x_ref, y0_ref, y1_ref, route_ref, vec_ref, *refs):
    w1, w2 = route_ref[:, 2:3], route_ref[:, 3:4]
    a0, b0 = _unpack_halves(y0_ref[...])
    a1, b1 = _unpack_halves(y1_ref[...])
    ffn = jnp.concatenate([a0 * w1 + a1 * w2, b0 * w1 + b1 * w2], axis=1)
    gate = vec_ref[_V_GATE:_V_GATE + 1, :]
    x1 = alpha * x_ref[...] + (1.0 + gate) * ffn
    xn = _layer_norm(x1, vec_ref[_V_GAIN:_V_GAIN + 1, :], vec_ref[_V_BIAS:_V_BIAS + 1, :])
    refs[0][...] = xn
    k = 1
    if with_kv:
        hk = xn * (1.0 + vec_ref[_V_SCALE_B:_V_SCALE_B + 1, :]) + vec_ref[_V_SHIFT_B:_V_SHIFT_B + 1, :]
        refs[k][...] = hk.astype(BF16)
        k += 1
    if with_next:
        h = xn * (1.0 + vec_ref[_V_SCALE_A:_V_SCALE_A + 1, :]) + vec_ref[_V_SHIFT_A:_V_SHIFT_A + 1, :]
        refs[k][...] = h.astype(BF16)


def _ln_combine(x2, y2, route, vecs, alpha, with_next, with_kv, tm=256):
    s, d = x2.shape
    row = lambda i: (i, 0)
    n_bf = int(with_kv) + int(with_next)
    shapes = [jax.ShapeDtypeStruct((s, d), F32)] + [jax.ShapeDtypeStruct((s, d), BF16)] * n_bf
    specs = [pl.BlockSpec((tm, d), row)] * (1 + n_bf)
    return pl.pallas_call(
        functools.partial(_ln_combine_kernel, alpha, with_next, with_kv),
        out_shape=shapes,
        grid=(s // tm,),
        in_specs=[pl.BlockSpec((tm, d), row),
                  pl.BlockSpec((tm, d // 2), row),
                  pl.BlockSpec((tm, d // 2), lambda i: (i + s // tm, 0)),
                  pl.BlockSpec((tm, LANES), row),
                  pl.BlockSpec((8, d), lambda i: (0, 0))],
        out_specs=specs,
        compiler_params=_params(("parallel",)),
        name="ln_combine",
    )(x2, y2, y2, route, vecs)


def _dilated_kernel(dil, blocks_per_stream, slopes, q_ref, kp_ref, kc_ref, vp_ref, vc_ref,
                    o_ref, lse_ref):
    nb = pl.program_id(0) % blocks_per_stream
    has_prev = nb > 0
    qi = lax.broadcasted_iota(jnp.int32, (A_BLOCK, A_BLOCK), 0)
    kj = lax.broadcasted_iota(jnp.int32, (A_BLOCK, A_BLOCK), 1)
    dist_prev = jnp.where((kj >= qi) & has_prev, ((qi + A_BLOCK - kj) * dil).astype(F32), -NEG)
    dist_cur = jnp.where(kj <= qi, ((qi - kj) * dil).astype(F32), -NEG)
    sqrt_e = math.sqrt(HEAD_DIM)
    lane = lax.broadcasted_iota(jnp.int32, (A_BLOCK, LANES), 1)
    ones = jnp.ones((A_BLOCK, LANES), BF16)
    n_heads = len(slopes)

    def scores(h):
        hs = slice(h * HEAD_DIM, (h + 1) * HEAD_DIM)
        q = q_ref[:, hs]
        return (lax.dot_general(q, kp_ref[:, hs], _NT, preferred_element_type=F32),
                lax.dot_general(q, kc_ref[:, hs], _NT, preferred_element_type=F32))

    ahead = 6
    pending = {h: scores(h) for h in range(min(ahead, n_heads))}
    lse_all = jnp.zeros((A_BLOCK, LANES), F32)
    for h, slope in enumerate(slopes):
        hs = slice(h * HEAD_DIM, (h + 1) * HEAD_DIM)
        s_p, s_c = pending.pop(h)
        t_p = s_p - (slope * sqrt_e) * dist_prev
        t_c = s_c - (slope * sqrt_e) * dist_cur
        t_max = jnp.max(jnp.maximum(t_p, t_c), axis=-1, keepdims=True)
        p_p = jnp.exp2((t_p - t_max) * (LOG2E / sqrt_e)).astype(BF16)
        p_c = jnp.exp2((t_c - t_max) * (LOG2E / sqrt_e)).astype(BF16)
        if h + ahead < n_heads:
            pending[h + ahead] = scores(h + ahead)
        o = jnp.dot(p_p, vp_ref[:, hs], preferred_element_type=F32) \
            + jnp.dot(p_c, vc_ref[:, hs], preferred_element_type=F32)
        l = jnp.dot(p_p, ones, preferred_element_type=F32) \
            + jnp.dot(p_c, ones, preferred_element_type=F32)
        o_ref[:, hs] = o * (1.0 / l)
        lse_all = jnp.where(lane == h, t_max * (1.0 / sqrt_e) + jnp.log(l), lse_all)
    lse_ref[...] = lse_all


def _dilated_attention(qkv, dil, n_heads):
    s = qkv.shape[0]
    w = n_heads * HEAD_DIM
    n = s // dil
    bps = n // A_BLOCK
    slopes = [float(v) for v in np.exp2(-8.0 * np.arange(1, n_heads + 1, dtype=np.float32) / n_heads)]
    prev = lambda b: (jnp.maximum(b - 1, 0), 1)
    o, lse = pl.pallas_call(
        functools.partial(_dilated_kernel, dil, bps, slopes),
        out_shape=[jax.ShapeDtypeStruct((n, dil * w), F32),
                   jax.ShapeDtypeStruct((n, dil * LANES), F32)],
        grid=(s // A_BLOCK,),
        in_specs=[pl.BlockSpec((A_BLOCK, w), lambda b: (b, 0)),
                  pl.BlockSpec((A_BLOCK, w), prev),
                  pl.BlockSpec((A_BLOCK, w), lambda b: (b, 1)),
                  pl.BlockSpec((A_BLOCK, w), lambda b: (jnp.maximum(b - 1, 0), 2)),
                  pl.BlockSpec((A_BLOCK, w), lambda b: (b, 2))],
        out_specs=[pl.BlockSpec((A_BLOCK, w), lambda b: (b % bps, b // bps)),
                   pl.BlockSpec((A_BLOCK, LANES), lambda b: (b % bps, b // bps))],
        compiler_params=_params(("parallel",)),
        name="dilated_attention",
    )(qkv, qkv, qkv, qkv, qkv)
    return o, lse


def _merge_kernel(n_heads, *refs):
    n_g = len(A_GROUPS)
    o_refs, l_refs = refs[:n_g], refs[n_g:2 * n_g]
    out_ref = refs[2 * n_g]
    stages = iter(refs[2 * n_g + 1:])
    tm = out_ref.shape[0]
    o_nat, l_nat = [], []
    for (_, dil), o_ref, l_ref in zip(A_GROUPS, o_refs, l_refs):
        if dil == 1:
            o_nat.append(lambda h, o_ref=o_ref: o_ref[:, h * HEAD_DIM:(h + 1) * HEAD_DIM])
            l_nat.append(l_ref[...])
            continue
        stage = next(stages)
        rows = tm // dil
        w = o_ref.shape[1] // dil
        for r in range(dil):
            for h in range(n_heads):
                stage[h, pl.ds(r, rows, stride=dil), :] = \
                    o_ref[:, r * w + h * HEAD_DIM:r * w + (h + 1) * HEAD_DIM]
            stage[n_heads, pl.ds(r, rows, stride=dil), :] = l_ref[:, r * LANES:(r + 1) * LANES]
        o_nat.append(lambda h, stage=stage: stage[h])
        l_nat.append(stage[n_heads])
    m = functools.reduce(jnp.maximum, l_nat)
    es = [jnp.exp(l - m) for l in l_nat]
    inv = 1.0 / functools.reduce(lambda a, b: a + b, es)
    ws = [e * inv for e in es]
    for h in range(n_heads):
        acc = functools.reduce(lambda a, b: a + b,
                               [o(h) * w_g[:, h:h + 1] for o, w_g in zip(o_nat, ws)])
        out_ref[:, h * HEAD_DIM:(h + 1) * HEAD_DIM] = acc.astype(BF16)


def _merge_groups(os_, lses, n_heads, tm=256):
    w = n_heads * HEAD_DIM
    s = os_[0].shape[0] * A_GROUPS[0][1]
    row = lambda i: (i, 0)
    dils = [dil for _, dil in A_GROUPS]
    return pl.pallas_call(
        functools.partial(_merge_kernel, n_heads),
        out_shape=jax.ShapeDtypeStruct((s, w), BF16),
        grid=(s // tm,),
        in_specs=[pl.BlockSpec((tm // dil, dil * w), row) for dil in dils]
        + [pl.BlockSpec((tm // dil, dil * LANES), row) for dil in dils],
        out_specs=pl.BlockSpec((tm, w), row),
        scratch_shapes=[pltpu.VMEM((n_heads + 1, tm, LANES), F32) for dil in dils if dil > 1],
        compiler_params=_params(("parallel",)),
        name="merge_groups",
    )(*os_, *lses)


def _kmean_kernel(k_ref, o_ref):
    o_ref[...] = jnp.mean(k_ref[...], axis=0, keepdims=True)


def _block_means(k):
    s, kw = k.shape
    nb = s // MOBA_BLOCK
    return pl.pallas_call(
        _kmean_kernel,
        out_shape=jax.ShapeDtypeStruct((nb, 1, kw), F32),
        grid=(nb,),
        in_specs=[pl.BlockSpec((MOBA_BLOCK, kw), lambda i: (i, 0))],
        out_specs=pl.BlockSpec((None, 1, kw), lambda i: (i, 0, 0)),
        compiler_params=_params(("parallel",)),
        name="block_means",
    )(k)


def _moba_kernel(q_ref, k_ref, vt_ref, kmean_ref, slope_ref, o_ref,
                 qs_ref, sel_ref, bias_ref, m_ref, l_ref, acc_ref):
    qb = pl.program_id(1)
    n_blk = kmean_ref.shape[0]
    rows = MOBA_REP * MOBA_BLOCK
    n_chunks = rows // LANES
    slope2 = slope_ref[...] * LOG2E

    @pl.when(qb == 0)
    def _():
        lane_t = (lax.broadcasted_iota(jnp.int32, (1, rows), 1) % MOBA_BLOCK).astype(F32)
        sub = lax.broadcasted_iota(jnp.int32, (MOBA_BLOCK, 1), 0).astype(F32)
        bias_ref[...] = slope2 * (lane_t - sub)

    q = q_ref[...]
    q4 = jnp.concatenate([q[:, r * HEAD_DIM:(r + 1) * HEAD_DIM] for r in range(MOBA_REP)], axis=0)
    gate = lax.dot_general(kmean_ref[...], q4, _NT, preferred_element_type=F32, precision=HIGHEST)
    blk = lax.broadcasted_iota(jnp.int32, (n_blk, rows), 0).astype(F32)
    ninf = -jnp.inf
    gate = jnp.where(blk < qb.astype(F32), gate, ninf)
    sel = jnp.zeros((n_blk, rows), F32)
    for _ in range(MOBA_TOPK):
        top = jnp.max(gate, axis=0, keepdims=True)
        idx = jnp.min(jnp.where(gate == top, blk, float(n_blk)), axis=0, keepdims=True)
        hit = (blk == idx) & (top > ninf)
        sel = jnp.where(hit, 1.0, sel)
        gate = jnp.where(hit, ninf, gate)
    sel_ref[...] = sel
    qs_ref[...] = (q4 * (LOG2E / math.sqrt(HEAD_DIM))).astype(BF16)

    kk = lax.broadcasted_iota(jnp.int32, (MOBA_BLOCK, LANES), 0)
    tt = lax.broadcasted_iota(jnp.int32, (MOBA_BLOCK, LANES), 1)

    def run_blocks(blocks, own):
        k_b = [k_ref[n] for n in blocks]
        vt_b = [vt_ref[n] for n in blocks]
        sel_b = [None if own else sel_ref[pl.ds(n, 1), :] for n in blocks]
        bt_b = [None if own else slope2 * ((n - qb) * MOBA_BLOCK).astype(F32) for n in blocks]
        tasks = [(bi, c) for bi in range(len(blocks)) for c in range(n_chunks)]

        def score(t):
            bi, c = tasks[t]
            return lax.dot_general(k_b[bi], qs_ref[c * LANES:(c + 1) * LANES, :], _NT,
                                   preferred_element_type=F32)

        ahead = 2
        pending = {t: score(t) for t in range(min(ahead, len(tasks)))}
        for t, (bi, c) in enumerate(tasks):
            cs = slice(c * LANES, (c + 1) * LANES)
            u = pending.pop(t) - bias_ref[:, cs]
            if own:
                u = jnp.where(kk <= tt + (c * LANES) % MOBA_BLOCK, u, NEG)
                m_new = jnp.max(u, axis=0, keepdims=True)
                p = jnp.exp2(u - m_new)
                l_new = jnp.sum(p, axis=0, keepdims=True)
            else:
                picked = sel_b[bi][:, cs] > 0.0
                bt = bt_b[bi][:, cs]
                m_old = m_ref[:, cs]
                m_new = jnp.maximum(
                    m_old, jnp.where(picked, jnp.max(u, axis=0, keepdims=True) + bt, NEG))
                p = jnp.exp2(u - jnp.where(picked, m_new - bt, -NEG))
                alpha = jnp.exp2(m_old - m_new)
                l_new = alpha * l_ref[:, cs] + jnp.sum(p, axis=0, keepdims=True)
            m_ref[:, cs] = m_new
            l_ref[:, cs] = l_new
            if t + ahead < len(tasks):
                pending[t + ahead] = score(t + ahead)
            pv = jnp.dot(vt_b[bi], p.astype(BF16), preferred_element_type=F32)
            acc_ref[:, cs] = pv if own else alpha * acc_ref[:, cs] + pv

    run_blocks([qb], own=True)

    def body(i, carry):
        run_blocks([i * MOBA_UNROLL + j for j in range(MOBA_UNROLL)], own=False)
        return carry

    lax.fori_loop(0, (qb + MOBA_UNROLL - 1) // MOBA_UNROLL, body, 0)
    o_t = acc_ref[...] * (1.0 / l_ref[...])
    for r in range(MOBA_REP):
        o_ref[:, r * HEAD_DIM:(r + 1) * HEAD_DIM] = \
            o_t[:, r * MOBA_BLOCK:(r + 1) * MOBA_BLOCK].T.astype(BF16)


def _moba_attention(q, k_blocks, vt_blocks, k_mean, slope_lanes):
    s = q.shape[0]
    g, nb = k_blocks.shape[:2]
    assert nb % MOBA_UNROLL == 0
    rows = MOBA_REP * MOBA_BLOCK
    qw = MOBA_REP * HEAD_DIM
    return pl.pallas_call(
        _moba_kernel,
        out_shape=jax.ShapeDtypeStruct((s, g * qw), BF16),
        grid=(g, nb),
        in_specs=[pl.BlockSpec((MOBA_BLOCK, qw), lambda gi, qb: (qb, gi)),
                  pl.BlockSpec((None, nb, MOBA_BLOCK, HEAD_DIM), lambda gi, qb: (gi, 0, 0, 0)),
                  pl.BlockSpec((None, nb, HEAD_DIM, MOBA_BLOCK), lambda gi, qb: (gi, 0, 0, 0)),
                  pl.BlockSpec((None, nb, HEAD_DIM), lambda gi, qb: (gi, 0, 0)),
                  pl.BlockSpec((None, 1, rows), lambda gi, qb: (gi, 0, 0))],
        out_specs=pl.BlockSpec((MOBA_BLOCK, qw), lambda gi, qb: (qb, gi)),
        scratch_shapes=[pltpu.VMEM((rows, HEAD_DIM), BF16),
                        pltpu.VMEM((nb, rows), F32),
                        pltpu.VMEM((MOBA_BLOCK, rows), F32),
                        pltpu.VMEM((1, rows), F32),
                        pltpu.VMEM((1, rows), F32),
                        pltpu.VMEM((HEAD_DIM, rows), F32)],
        compiler_params=_params(("parallel", "arbitrary")),
        name="moba_attention",
    )(q, k_blocks, vt_blocks, k_mean, slope_lanes)


def _moe_kernel(blk_e_ref, total_ref, blk_first_ref, slot_tok_ref, slot_dst_ref,
                x_hbm, wg_ref, wu_ref, wd_ref, y_hbm,
                xbuf, obuf, wg_bf, wu_bf, wd_bf, gsem, ssem):
    b = pl.program_id(0)
    total = total_ref[0]
    slot = b % 2
    dp = xbuf.shape[2]
    chunk = dp // MOE_CHUNKS

    def gather_copy(blk, j, buf_slot):
        tok = slot_tok_ref[blk * MOE_BLOCK + j]
        return pltpu.make_async_copy(x_hbm.at[pl.ds(tok, 1)], xbuf.at[buf_slot, pl.ds(j, 1)],
                                     gsem.at[buf_slot])

    def scatter_copy(blk, j, buf_slot):
        dst = slot_dst_ref[(blk + 1) * MOE_BLOCK + j]
        return pltpu.make_async_copy(obuf.at[buf_slot, pl.ds(j, 1)], y_hbm.at[pl.ds(dst, 1)],
                                     ssem.at[buf_slot])

    def wait_gather(buf_slot):
        pltpu.make_async_copy(x_hbm.at[pl.ds(0, MOE_BLOCK)], xbuf.at[buf_slot],
                              gsem.at[buf_slot]).wait()

    def wait_scatter(buf_slot):
        pltpu.make_async_copy(obuf.at[buf_slot], y_hbm.at[pl.ds(0, MOE_BLOCK)],
                              ssem.at[buf_slot]).wait()

    def issue_rows(lo, hi):
        for j in range(lo, hi):
            gather_copy(b + 1, j, 1 - slot).start()
            scatter_copy(b - 1, j, 1 - slot).start()

    @pl.when(b == 0)
    def _():
        n_real = y_hbm.shape[0] - 2 * MOE_BLOCK
        obuf[...] = jnp.zeros(obuf.shape, U32)
        for i in range(2):
            dump = y_hbm.at[pl.ds(n_real + i * MOE_BLOCK, MOE_BLOCK)]
            pltpu.make_async_copy(obuf.at[0], dump, ssem.at[i]).start()
        for i in range(2):
            wait_scatter(i)
        for j in range(MOE_BLOCK):
            gather_copy(0, j, 0).start()

    @pl.when(b < total)
    def _():
        @pl.when(blk_first_ref[b] == 1)
        def _():
            wg_bf[...] = wg_ref[...].astype(BF16)
            wu_bf[...] = wu_ref[...].astype(BF16)
            wd_bf[...] = wd_ref[...].astype(BF16)

        wait_gather(slot)

        @pl.when(b >= 1)
        def _():
            wait_scatter(slot)

        rows_a = MOE_BLOCK // 2 // (2 * MOE_CHUNKS)
        gt = up = None
        for c in range(2 * MOE_CHUNKS):
            issue_rows(c * rows_a, (c + 1) * rows_a)
            words = xbuf[slot, :, (c % MOE_CHUNKS) * chunk:(c % MOE_CHUNKS + 1) * chunk]
            half = _unpack_halves(words)[c // MOE_CHUNKS].astype(BF16)
            k0 = c * chunk
            g_c = jnp.dot(half, wg_bf[k0:k0 + chunk, :], preferred_element_type=F32)
            u_c = jnp.dot(half, wu_bf[k0:k0 + chunk, :], preferred_element_type=F32)
            gt = g_c if gt is None else gt + g_c
            up = u_c if up is None else up + u_c
        hid = (gt * (1.0 / (1.0 + jnp.exp(-gt))) * up).astype(BF16)
        rows_b = MOE_BLOCK // 2 // MOE_CHUNKS
        base = MOE_BLOCK // 2
        for c in range(MOE_CHUNKS):
            issue_rows(base + c * rows_b, base + (c + 1) * rows_b)
            lo = jnp.dot(hid, wd_bf[:, c * chunk:(c + 1) * chunk], preferred_element_type=F32)
            hi = jnp.dot(hid, wd_bf[:, dp + c * chunk:dp + (c + 1) * chunk],
                         preferred_element_type=F32)
            obuf[slot, :, c * chunk:(c + 1) * chunk] = \
                _pack_halves(jnp.concatenate([lo, hi], axis=1))

        @pl.when(b == total - 1)
        def _():
            for j in range(MOE_BLOCK):
                scatter_copy(b, j, slot).start()
            wait_scatter(1 - slot)
            wait_scatter(slot)
            wait_gather(1 - slot)


def _moe_dispatch(ids, n_experts):
    s, k = ids.shape
    a = s * k
    n_blocks = -(-a // MOE_BLOCK) + n_experts
    e_flat = ids.reshape(a)
    order = jnp.argsort(e_flat, stable=True).astype(jnp.int32)
    experts = jnp.arange(n_experts, dtype=jnp.int32)
    counts = jnp.sum((e_flat[:, None] == experts[None, :]).astype(jnp.int32), axis=0)
    starts = jnp.cumsum(counts) - counts
    nblk = (counts + MOE_BLOCK - 1) // MOE_BLOCK
    blk_end = jnp.cumsum(nblk)
    blk_start = blk_end - nblk
    total = blk_end[-1]
    bidx = jnp.arange(n_blocks, dtype=jnp.int32)
    used = bidx < total
    b_eff = jnp.minimum(bidx, total - 1)
    blk_e = jnp.sum((blk_end[None, :] <= b_eff[:, None]).astype(jnp.int32), axis=1)
    blk_e = jnp.minimum(blk_e, n_experts - 1)
    rank0 = (bidx - blk_start[blk_e]) * MOE_BLOCK
    blk_first = (used & (rank0 == 0)).astype(jnp.int32)
    lane = jnp.arange(MOE_BLOCK, dtype=jnp.int32)[None, :]
    rank = rank0[:, None] + lane
    valid = used[:, None] & (rank < counts[blk_e][:, None])
    src = jnp.clip(starts[blk_e][:, None] + rank, 0, a - 1)
    asg = order[src]
    slot_tok = jnp.where(valid, asg // k, 0)
    slot_tok = jnp.concatenate([slot_tok, jnp.zeros((1, MOE_BLOCK), jnp.int32)]).reshape(-1)
    dump = a + (bidx[:, None] % 2) * MOE_BLOCK + lane
    slot_dst = jnp.where(valid, (asg % k) * s + asg // k, dump)
    slot_dst = jnp.concatenate([a + MOE_BLOCK + lane, slot_dst]).reshape(-1)
    return blk_e, total.reshape(1), blk_first, slot_tok, slot_dst, n_blocks


def _moe_experts(hfp, ids, layer, w_gate, w_up, w_down):
    s, dp = hfp.shape
    d = 2 * dp
    n_experts, _, f = w_gate.shape[1:]
    k = ids.shape[1]
    blk_e, total, blk_first, slot_tok, slot_dst, n_blocks = _moe_dispatch(ids, n_experts)
    w_idx = lambda b, be, *_: (layer, be[b], 0, 0)
    return pl.pallas_call(
        _moe_kernel,
        out_shape=jax.ShapeDtypeStruct((s * k + 2 * MOE_BLOCK, dp), U32),
        grid_spec=pltpu.PrefetchScalarGridSpec(
            num_scalar_prefetch=5,
            grid=(n_blocks,),
            in_specs=[pl.BlockSpec(memory_space=pl.ANY),
                      pl.BlockSpec((None, None, d, f), w_idx),
                      pl.BlockSpec((None, None, d, f), w_idx),
                      pl.BlockSpec((None, None, f, d), w_idx)],
            out_specs=pl.BlockSpec(memory_space=pl.ANY),
            scratch_shapes=[pltpu.VMEM((2, MOE_BLOCK, dp), U32),
                            pltpu.VMEM((2, MOE_BLOCK, dp), U32),
                            pltpu.VMEM((d, f), BF16),
                            pltpu.VMEM((d, f), BF16),
                            pltpu.VMEM((f, d), BF16),
                            pltpu.SemaphoreType.DMA((2,)),
                            pltpu.SemaphoreType.DMA((2,))]),
        compiler_params=_params(("arbitrary",)),
        name="moe_experts",
    )(blk_e, total, blk_first, slot_tok, slot_dst, hfp, w_gate, w_up, w_down)


def _vec_pack(d, **rows):
    out = jnp.zeros((8, d), F32)
    for name, v in rows.items():
        out = out.at[name_to_row[name]].set(v)
    return out


name_to_row = {"gate": _V_GATE, "gain": _V_GAIN, "bias": _V_BIAS, "scale_a": _V_SCALE_A,
               "shift_a": _V_SHIFT_A, "scale_b": _V_SCALE_B, "shift_b": _V_SHIFT_B}


def kernel(x, c, ada_down, ada_up, ada_bias, ln_gain, ln_bias, a_w_qkv, a_w_o, kv_ada_down, kv_ada_up, kv_ada_bias, kv_w, b_w_q, b_w_o, moe_w_group, moe_b_group, moe_w_expert, moe_b_expert, moe_w_gate, moe_w_up, moe_w_down):
    batch, s, d = x.shape
    assert batch == 1
    depth = ada_down.shape[0]
    n_a = a_w_qkv.shape[0]
    alpha = (2 * depth) ** 0.25
    a_heads = a_w_o.shape[1] // HEAD_DIM
    a_width = a_heads * HEAD_DIM
    n_groups = moe_w_group.shape[-1]
    n_per_group = moe_w_expert.shape[-1]
    kv_heads = kv_w.shape[1] // (2 * HEAD_DIM)
    kv_width = kv_heads * HEAD_DIM
    b_heads = kv_heads * MOBA_REP
    n_blk = s // MOBA_BLOCK

    mods = _modulation(c, ada_down, ada_up, ada_bias).reshape(depth, 6, d)
    kv_mod = _modulation(c, kv_ada_down[None], kv_ada_up[None], kv_ada_bias[None]).reshape(2, d)

    x2 = x.reshape(s, d)
    hs = _prologue(x2, _vec_pack(d, scale_a=mods[0, 1], shift_a=mods[0, 0]))

    k_blocks = vt_blocks = k_mean = slope_lanes = None
    for l in range(depth):
        shift1, scale1, gate1, shift2, scale2, gate2 = [mods[l, i] for i in range(6)]
        if l < n_a:
            outs, lses = [], []
            for g, (_, dil) in enumerate(A_GROUPS):
                qkv = _matmul(hs, a_w_qkv, l, BF16, n_off=g * 3 * a_width, n_out=3 * a_width,
                              dil=dil)
                o_g, lse_g = _dilated_attention(qkv, dil, a_heads)
                outs.append(o_g)
                lses.append(lse_g)
            merged = _merge_groups(outs, lses, a_heads)
            mix = _matmul(merged, a_w_o, l, BF16)
        else:
            j = l - n_a
            q = _matmul(hs, b_w_q, j, F32)
            o = _moba_attention(q, k_blocks, vt_blocks, k_mean, slope_lanes)
            mix = _matmul(o, b_w_o, j, BF16)

        wr = jnp.concatenate(
            [moe_w_group[l], jnp.moveaxis(moe_w_expert[l], 0, 1).reshape(d, n_groups * n_per_group),
             jnp.zeros((d, LANES - n_groups * (1 + n_per_group)), F32)], axis=1)
        br = jnp.concatenate(
            [moe_b_group[l], moe_b_expert[l].reshape(-1),
             jnp.zeros((LANES - n_groups * (1 + n_per_group),), F32)]).reshape(1, LANES)
        vec1 = _vec_pack(d, gate=gate1, gain=ln_gain[l, 0], bias=ln_bias[l, 0],
                         scale_a=scale2, shift_a=shift2)
        x2, route, hfp = _ln_router(x2, mix, vec1, wr, br, alpha, n_groups, n_per_group)
        ids = route[:, :MOE_TOPK].astype(jnp.int32)
        y2 = _moe_experts(hfp, ids, l, moe_w_gate, moe_w_up, moe_w_down)

        last = l == depth - 1
        with_kv = l == n_a - 1 and n_a < depth
        rows = dict(gate=gate2, gain=ln_gain[l, 1], bias=ln_bias[l, 1])
        if not last:
            rows.update(scale_a=mods[l + 1, 1], shift_a=mods[l + 1, 0])
        if with_kv:
            rows.update(scale_b=kv_mod[1], shift_b=kv_mod[0])
        res = _ln_combine(x2, y2, route, _vec_pack(d, **rows), alpha, not last, with_kv)
        x2 = res[0]
        if with_kv:
            kv = _matmul(res[1], kv_w[None], 0, F32)
            k_f32 = kv[:, :kv_width]
            k_mean = _block_means(k_f32).reshape(n_blk, kv_heads, HEAD_DIM).transpose(1, 0, 2)
            k_blocks = k_f32.astype(BF16).reshape(n_blk, MOBA_BLOCK, kv_heads, HEAD_DIM) \
                .transpose(2, 0, 1, 3)
            vt_blocks = kv[:, kv_width:].astype(BF16).reshape(n_blk, MOBA_BLOCK, kv_heads, HEAD_DIM) \
                .transpose(2, 0, 3, 1)
            slopes = jnp.exp2(-8.0 * jnp.arange(1, b_heads + 1, dtype=F32) / b_heads)
            slope_lanes = jnp.repeat(slopes.reshape(kv_heads, MOBA_REP), MOBA_BLOCK, axis=1) \
                .reshape(kv_heads, 1, MOBA_REP * MOBA_BLOCK)
        if not last:
            hs = res[-1]
    return x2.reshape(batch, s, d)
```

```python
import functools
import math

import numpy as np
import jax
import jax.numpy as jnp
from jax import lax
from jax.experimental import pallas as pl
from jax.experimental.pallas import tpu as pltpu

F32 = jnp.float32
BF16 = jnp.bfloat16
U32 = jnp.uint32
HIGHEST = lax.Precision.HIGHEST

HEAD_DIM = 128
A_GROUPS = ((128, 1), (512, 4), (2048, 16))
A_BLOCK = 128
MOBA_BLOCK = 256
MOBA_TOPK = 3
MOBA_REP = 4
MOBA_UNROLL = 4
MOE_TOPK = 2
MOE_BLOCK = 128
LN_EPS = 1e-5
NEG = -1e30
LOG2E = math.log2(math.e)
LANES = 128
VMEM_LIMIT = 56 * 1024 * 1024

_NT = (((1,), (1,)), ((), ()))


def _params(sem, vmem=VMEM_LIMIT):
    return pltpu.CompilerParams(dimension_semantics=sem, vmem_limit_bytes=vmem)


def _mod_kernel(c_ref, wd_ref, wu_ref, b_ref, o_ref, t_ref):
    @pl.when(pl.program_id(1) == 0)
    def _():
        c = c_ref[...]
        sc = c * (1.0 / (1.0 + jnp.exp(-c)))
        t_ref[...] = jnp.dot(sc, wd_ref[...], preferred_element_type=F32, precision=HIGHEST)

    o_ref[...] = jnp.dot(t_ref[...], wu_ref[...], preferred_element_type=F32,
                         precision=HIGHEST) + b_ref[...]


def _modulation(c, w_down, w_up, b_up):
    n_l, d, r = w_down.shape
    n = w_up.shape[-1]
    tn = n // pl.cdiv(n, 6144)
    c8 = jnp.broadcast_to(c, (8, d))
    out = pl.pallas_call(
        _mod_kernel,
        out_shape=jax.ShapeDtypeStruct((n_l, 8, n), F32),
        grid=(n_l, n // tn),
        in_specs=[
            pl.BlockSpec((8, d), lambda l, j: (0, 0)),
            pl.BlockSpec((None, d, r), lambda l, j: (l, 0, 0)),
            pl.BlockSpec((None, r, tn), lambda l, j: (l, 0, j)),
            pl.BlockSpec((None, 1, tn), lambda l, j: (l, 0, j)),
        ],
        out_specs=pl.BlockSpec((None, 8, tn), lambda l, j: (l, 0, j)),
        scratch_shapes=[pltpu.VMEM((8, r), F32)],
        compiler_params=_params(("arbitrary", "arbitrary")),
        name="modulation",
    )(c8, w_down, w_up, b_up.reshape(n_l, 1, n))
    return out[:, 0, :]


def _mm_kernel(a_ref, b_ref, o_ref, b_bf_ref):
    @pl.when(pl.program_id(1) == 0)
    def _():
        b_bf_ref[...] = b_ref[...].astype(BF16)

    o_ref[...] = jnp.dot(a_ref[...], b_bf_ref[...],
                         preferred_element_type=F32).astype(o_ref.dtype)


def _matmul(a, b, layer, out_dtype, n_off=0, n_out=None, dil=1, tm=1024, tn=512):
    m, k = a.shape
    n_out = b.shape[2] if n_out is None else n_out
    tm = min(tm, m // dil)
    tn = min(tn, n_out)
    off = n_off // tn
    bps = m // dil // tm
    return pl.pallas_call(
        _mm_kernel,
        out_shape=jax.ShapeDtypeStruct((m, n_out), out_dtype),
        grid=(n_out // tn, m // tm),
        in_specs=[
            pl.BlockSpec((tm, k), lambda j, i: (i % bps, i // bps)),
            pl.BlockSpec((None, k, tn), lambda j, i: (layer, 0, j + off)),
        ],
        out_specs=pl.BlockSpec((tm, tn), lambda j, i: (i, j)),
        scratch_shapes=[pltpu.VMEM((k, tn), BF16)],
        compiler_params=_params(("parallel", "arbitrary")),
        name="matmul",
    )(a.reshape(m // dil, dil * k), b)


def _pack_halves(v):
    n = v.shape[1] // 2
    hi = lax.bitcast_convert_type(v[:, :n].astype(BF16).astype(F32), U32)
    lo = lax.bitcast_convert_type(v[:, n:].astype(BF16).astype(F32), U32)
    return (hi & jnp.uint32(0xFFFF0000)) | (lo >> 16)


def _unpack_halves(p):
    return (lax.bitcast_convert_type(p & jnp.uint32(0xFFFF0000), F32),
            lax.bitcast_convert_type(p << 16, F32))


def _layer_norm(v, gain, bias):
    mu = jnp.mean(v, axis=-1, keepdims=True)
    vc = v - mu
    var = jnp.mean(vc * vc, axis=-1, keepdims=True)
    return vc * lax.rsqrt(var + LN_EPS) * gain + bias


_V_GATE, _V_GAIN, _V_BIAS, _V_SCALE_A, _V_SHIFT_A, _V_SCALE_B, _V_SHIFT_B = range(7)


def _prologue_kernel(x_ref, vec_ref, o_ref):
    h = x_ref[...] * (1.0 + vec_ref[_V_SCALE_A:_V_SCALE_A + 1, :]) \
        + vec_ref[_V_SHIFT_A:_V_SHIFT_A + 1, :]
    o_ref[...] = h.astype(BF16)


def _prologue(x2, vecs, tm=256):
    s, d = x2.shape
    return pl.pallas_call(
        _prologue_kernel,
        out_shape=jax.ShapeDtypeStruct((s, d), BF16),
        grid=(s // tm,),
        in_specs=[pl.BlockSpec((tm, d), lambda i: (i, 0)),
                  pl.BlockSpec((8, d), lambda i: (0, 0))],
        out_specs=pl.BlockSpec((tm, d), lambda i: (i, 0)),
        compiler_params=_params(("parallel",)),
        name="prologue",
    )(x2, vecs)


def _ln_router_kernel(alpha, n_groups, n_per_group,
                      x_ref, mix_ref, vec_ref, wrh_ref, wrl_ref, br_ref, xo_ref, route_ref, hfp_ref):
    gate = vec_ref[_V_GATE:_V_GATE + 1, :]
    x1 = alpha * x_ref[...] + (1.0 + gate) * mix_ref[...].astype(F32)
    xn = _layer_norm(x1, vec_ref[_V_GAIN:_V_GAIN + 1, :], vec_ref[_V_BIAS:_V_BIAS + 1, :])
    xo_ref[...] = xn
    hf = xn * (1.0 + vec_ref[_V_SCALE_A:_V_SCALE_A + 1, :]) + vec_ref[_V_SHIFT_A:_V_SHIFT_A + 1, :]
    hfp_ref[...] = _pack_halves(hf)
    hf_hi = hf.astype(BF16)
    hf_lo = (hf - hf_hi.astype(F32)).astype(BF16)
    logits = jnp.dot(hf_hi, wrh_ref[...], preferred_element_type=F32) \
        + jnp.dot(hf_hi, wrl_ref[...], preferred_element_type=F32) \
        + jnp.dot(hf_lo, wrh_ref[...], preferred_element_type=F32) + br_ref[...]
    lane = lax.broadcasted_iota(jnp.int32, logits.shape, 1).astype(F32)
    ninf = -jnp.inf
    big = float(LANES)
    is_g = lane < n_groups
    gl = jnp.where(is_g, logits, ninf)
    gmax = jnp.max(gl, axis=-1, keepdims=True)
    gsel = jnp.min(jnp.where(gl == gmax, lane, big), axis=-1, keepdims=True)
    p_group = 1.0 / jnp.sum(jnp.where(is_g, jnp.exp(logits - gmax), 0.0), axis=-1, keepdims=True)
    lo = n_groups + gsel * n_per_group
    el = jnp.where((lane >= lo) & (lane < lo + n_per_group), logits, ninf)
    m1 = jnp.max(el, axis=-1, keepdims=True)
    i1 = jnp.min(jnp.where(el == m1, lane, big), axis=-1, keepdims=True)
    el2 = jnp.where(lane == i1, ninf, el)
    m2 = jnp.max(el2, axis=-1, keepdims=True)
    i2 = jnp.min(jnp.where(el2 == m2, lane, big), axis=-1, keepdims=True)
    e21 = jnp.exp(m2 - m1)
    inv = 1.0 / (1.0 + e21)
    w1 = p_group * inv
    w2 = p_group * (e21 * inv)
    route_ref[...] = jnp.where(lane == 0.0, i1 - n_groups,
                     jnp.where(lane == 1.0, i2 - n_groups,
                     jnp.where(lane == 2.0, w1,
                     jnp.where(lane == 3.0, w2, 0.0))))


def _ln_router(x2, mix, vecs, wr, br, alpha, n_groups, n_per_group, tm=256):
    s, d = x2.shape
    wr_top = lax.bitcast_convert_type(
        lax.bitcast_convert_type(wr, U32) & jnp.uint32(0xFFFF0000), F32)
    wr_hi = wr_top.astype(BF16)
    wr_lo = (wr - wr_top).astype(BF16)
    return pl.pallas_call(
        functools.partial(_ln_router_kernel, alpha, n_groups, n_per_group),
        out_shape=[jax.ShapeDtypeStruct((s, d), F32),
                   jax.ShapeDtypeStruct((s, LANES), F32),
                   jax.ShapeDtypeStruct((s, d // 2), U32)],
        grid=(s // tm,),
        in_specs=[pl.BlockSpec((tm, d), lambda i: (i, 0)),
                  pl.BlockSpec((tm, d), lambda i: (i, 0)),
                  pl.BlockSpec((8, d), lambda i: (0, 0)),
                  pl.BlockSpec((d, LANES), lambda i: (0, 0)),
                  pl.BlockSpec((d, LANES), lambda i: (0, 0)),
                  pl.BlockSpec((1, LANES), lambda i: (0, 0))],
        out_specs=[pl.BlockSpec((tm, d), lambda i: (i, 0)),
                   pl.BlockSpec((tm, LANES), lambda i: (i, 0)),
                   pl.BlockSpec((tm, d // 2), lambda i: (i, 0))],
        compiler_params=_params(("parallel",)),
        name="ln_router",
    )(x2, mix, vecs, wr_hi, wr_lo, br)


def _ln_combine_kernel(alpha, with_next, with_kv, x_ref, y0_ref, y1_ref, route_ref, vec_ref, *refs):
    w1, w2 = route_ref[:, 2:3], route_ref[:, 3:4]
    a0, b0 = _unpack_halves(y0_ref[...])
    a1, b1 = _unpack_halves(y1_ref[...])
    ffn = jnp.concatenate([a0 * w1 + a1 * w2, b0 * w1 + b1 * w2], axis=1)
    gate = vec_ref[_V_GATE:_V_GATE + 1, :]
    x1 = alpha * x_ref[...] + (1.0 + gate) * ffn
    xn = _layer_norm(x1, vec_ref[_V_GAIN:_V_GAIN + 1, :], vec_ref[_V_BIAS:_V_BIAS + 1, :])
    refs[0][...] = xn
    k = 1
    if with_kv:
        hk = xn * (1.0 + vec_ref[_V_SCALE_B:_V_SCALE_B + 1, :]) + vec_ref[_V_SHIFT_B:_V_SHIFT_B + 1, :]
        refs[k][...] = hk.astype(BF16)
        k += 1
    if with_next:
        h = xn * (1.0 + vec_ref[_V_SCALE_A:_V_SCALE_A + 1, :]) + vec_ref[_V_SHIFT_A:_V_SHIFT_A + 1, :]
        refs[k][...] = h.astype(BF16)


def _ln_combine(x2, y2, route, vecs, alpha, with_next, with_kv, tm=256):
    s, d = x2.shape
    row = lambda i: (i, 0)
    n_bf = int(with_kv) + int(with_next)
    shapes = [jax.ShapeDtypeStruct((s, d), F32)] + [jax.ShapeDtypeStruct((s, d), BF16)] * n_bf
    specs = [pl.BlockSpec((tm, d), row)] * (1 + n_bf)
    return pl.pallas_call(
        functools.partial(_ln_combine_kernel, alpha, with_next, with_kv),
        out_shape=shapes,
        grid=(s // tm,),
        in_specs=[pl.BlockSpec((tm, d), row),
                  pl.BlockSpec((tm, d // 2), row),
                  pl.BlockSpec((tm, d // 2), lambda i: (i + s // tm, 0)),
                  pl.BlockSpec((tm, LANES), row),
                  pl.BlockSpec((8, d), lambda i: (0, 0))],
        out_specs=specs,
        compiler_params=_params(("parallel",)),
        name="ln_combine",
    )(x2, y2, y2, route, vecs)


def _dilated_kernel(dil, blocks_per_stream, slopes, q_ref, kp_ref, kc_ref, vp_ref, vc_ref,
                    o_ref, lse_ref):
    nb = pl.program_id(0) % blocks_per_stream
    has_prev = nb > 0
    qi = lax.broadcasted_iota(jnp.int32, (A_BLOCK, A_BLOCK), 0)
    kj = lax.broadcasted_iota(jnp.int32, (A_BLOCK, A_BLOCK), 1)
    dist_prev = jnp.where((kj >= qi) & has_prev, ((qi + A_BLOCK - kj) * dil).astype(F32), -NEG)
    dist_cur = jnp.where(kj <= qi, ((qi - kj) * dil).astype(F32), -NEG)
    sqrt_e = math.sqrt(HEAD_DIM)
    lane = lax.broadcasted_iota(jnp.int32, (A_BLOCK, LANES), 1)
    ones = jnp.ones((A_BLOCK, LANES), BF16)
    n_heads = len(slopes)

    def scores(h):
        hs = slice(h * HEAD_DIM, (h + 1) * HEAD_DIM)
        q = q_ref[:, hs]
        return (lax.dot_general(q, kp_ref[:, hs], _NT, preferred_element_type=F32),
                lax.dot_general(q, kc_ref[:, hs], _NT, preferred_element_type=F32))

    ahead = 6
    pending = {h: scores(h) for h in range(min(ahead, n_heads))}
    lse_all = jnp.zeros((A_BLOCK, LANES), F32)
    for h, slope in enumerate(slopes):
        hs = slice(h * HEAD_DIM, (h + 1) * HEAD_DIM)
        s_p, s_c = pending.pop(h)
        t_p = s_p - (slope * sqrt_e) * dist_prev
        t_c = s_c - (slope * sqrt_e) * dist_cur
        t_max = jnp.max(jnp.maximum(t_p, t_c), axis=-1, keepdims=True)
        p_p = jnp.exp2((t_p - t_max) * (LOG2E / sqrt_e)).astype(BF16)
        p_c = jnp.exp2((t_c - t_max) * (LOG2E / sqrt_e)).astype(BF16)
        if h + ahead < n_heads:
            pending[h + ahead] = scores(h + ahead)
        o = jnp.dot(p_p, vp_ref[:, hs], preferred_element_type=F32) \
            + jnp.dot(p_c, vc_ref[:, hs], preferred_element_type=F32)
        l = jnp.dot(p_p, ones, preferred_element_type=F32) \
            + jnp.dot(p_c, ones, preferred_element_type=F32)
        o_ref[:, hs] = o * (1.0 / l)
        lse_all = jnp.where(lane == h, t_max * (1.0 / sqrt_e) + jnp.log(l), lse_all)
    lse_ref[...] = lse_all


def _dilated_attention(qkv, dil, n_heads):
    s = qkv.shape[0]
    w = n_heads * HEAD_DIM
    n = s // dil
    bps = n // A_BLOCK
    slopes = [float(v) for v in np.exp2(-8.0 * np.arange(1, n_heads + 1, dtype=np.float32) / n_heads)]
    prev = lambda b: (jnp.maximum(b - 1, 0), 1)
    o, lse = pl.pallas_call(
        functools.partial(_dilated_kernel, dil, bps, slopes),
        out_shape=[jax.ShapeDtypeStruct((n, dil * w), F32),
                   jax.ShapeDtypeStruct((n, dil * LANES), F32)],
        grid=(s // A_BLOCK,),
        in_specs=[pl.BlockSpec((A_BLOCK, w), lambda b: (b, 0)),
                  pl.BlockSpec((A_BLOCK, w), prev),
                  pl.BlockSpec((A_BLOCK, w), lambda b: (b, 1)),
                  pl.BlockSpec((A_BLOCK, w), lambda b: (jnp.maximum(b - 1, 0), 2)),
                  pl.BlockSpec((A_BLOCK, w), lambda b: (b, 2))],
        out_specs=[pl.BlockSpec((A_BLOCK, w), lambda b: (b % bps, b // bps)),
                   pl.BlockSpec((A_BLOCK, LANES), lambda b: (b % bps, b // bps))],
        compiler_params=_params(("parallel",)),
        name="dilated_attention",
    )(qkv, qkv, qkv, qkv, qkv)
    return o, lse


def _merge_kernel(n_heads, *refs):
    n_g = len(A_GROUPS)
    o_refs, l_refs = refs[:n_g], refs[n_g:2 * n_g]
    out_ref = refs[2 * n_g]
    stages = iter(refs[2 * n_g + 1:])
    tm = out_ref.shape[0]
    o_nat, l_nat = [], []
    for (_, dil), o_ref, l_ref in zip(A_GROUPS, o_refs, l_refs):
        if dil == 1:
            o_nat.append(lambda h, o_ref=o_ref: o_ref[:, h * HEAD_DIM:(h + 1) * HEAD_DIM])
            l_nat.append(l_ref[...])
            continue
        stage = next(stages)
        rows = tm // dil
        w = o_ref.shape[1] // dil
        for r in range(dil):
            for h in range(n_heads):
                stage[h, pl.ds(r, rows, stride=dil), :] = \
                    o_ref[:, r * w + h * HEAD_DIM:r * w + (h + 1) * HEAD_DIM]
            stage[n_heads, pl.ds(r, rows, stride=dil), :] = l_ref[:, r * LANES:(r + 1) * LANES]
        o_nat.append(lambda h, stage=stage: stage[h])
        l_nat.append(stage[n_heads])
    m = functools.reduce(jnp.maximum, l_nat)
    es = [jnp.exp(l - m) for l in l_nat]
    inv = 1.0 / functools.reduce(lambda a, b: a + b, es)
    ws = [e * inv for e in es]
    for h in range(n_heads):
        acc = functools.reduce(lambda a, b: a + b,
                               [o(h) * w_g[:, h:h + 1] for o, w_g in zip(o_nat, ws)])
        out_ref[:, h * HEAD_DIM:(h + 1) * HEAD_DIM] = acc.astype(BF16)


def _merge_groups(os_, lses, n_heads, tm=256):
    w = n_heads * HEAD_DIM
    s = os_[0].shape[0] * A_GROUPS[0][1]
    row = lambda i: (i, 0)
    dils = [dil for _, dil in A_GROUPS]
    return pl.pallas_call(
        functools.partial(_merge_kernel, n_heads),
        out_shape=jax.ShapeDtypeStruct((s, w), BF16),
        grid=(s // tm,),
        in_specs=[pl.BlockSpec((tm // dil, dil * w), row) for dil in dils]
        + [pl.BlockSpec((tm // dil, dil * LANES), row) for dil in dils],
        out_specs=pl.BlockSpec((tm, w), row),
        scratch_shapes=[pltpu.VMEM((n_heads + 1, tm, LANES), F32) for dil in dils if dil > 1],
        compiler_params=_params(("parallel",)),
        name="merge_groups",
    )(*os_, *lses)


def _kmean_kernel(k_ref, o_ref):
    o_ref[...] = jnp.mean(k_ref[...], axis=0, keepdims=True)


def _block_means(k):
    s, kw = k.shape
    nb = s // MOBA_BLOCK
    return pl.pallas_call(
        _kmean_kernel,
        out_shape=jax.ShapeDtypeStruct((nb, 1, kw), F32),
        grid=(nb,),
        in_specs=[pl.BlockSpec((MOBA_BLOCK, kw), lambda i: (i, 0))],
        out_specs=pl.BlockSpec((None, 1, kw), lambda i: (i, 0, 0)),
        compiler_params=_params(("parallel",)),
        name="block_means",
    )(k)


def _moba_kernel(q_ref, k_ref, vt_ref, kmean_ref, slope_ref, o_ref,
                 qs_ref, sel_ref, bias_ref, m_ref, l_ref, acc_ref):
    qb = pl.program_id(1)
    n_blk = kmean_ref.shape[0]
    rows = MOBA_REP * MOBA_BLOCK
    n_chunks = rows // LANES
    slope2 = slope_ref[...] * LOG2E

    @pl.when(qb == 0)
    def _():
        lane_t = (lax.broadcasted_iota(jnp.int32, (1, rows), 1) % MOBA_BLOCK).astype(F32)
        sub = lax.broadcasted_iota(jnp.int32, (MOBA_BLOCK, 1), 0).astype(F32)
        bias_ref[...] = slope2 * (lane_t - sub)

    q = q_ref[...]
    q4 = jnp.concatenate([q[:, r * HEAD_DIM:(r + 1) * HEAD_DIM] for r in range(MOBA_REP)], axis=0)
    gate = lax.dot_general(kmean_ref[...], q4, _NT, preferred_element_type=F32, precision=HIGHEST)
    blk = lax.broadcasted_iota(jnp.int32, (n_blk, rows), 0).astype(F32)
    ninf = -jnp.inf
    gate = jnp.where(blk < qb.astype(F32), gate, ninf)
    sel = jnp.zeros((n_blk, rows), F32)
    for _ in range(MOBA_TOPK):
        top = jnp.max(gate, axis=0, keepdims=True)
        idx = jnp.min(jnp.where(gate == top, blk, float(n_blk)), axis=0, keepdims=True)
        hit = (blk == idx) & (top > ninf)
        sel = jnp.where(hit, 1.0, sel)
        gate = jnp.where(hit, ninf, gate)
    sel_ref[...] = sel
    qs_ref[...] = (q4 * (LOG2E / math.sqrt(HEAD_DIM))).astype(BF16)

    kk = lax.broadcasted_iota(jnp.int32, (MOBA_BLOCK, LANES), 0)
    tt = lax.broadcasted_iota(jnp.int32, (MOBA_BLOCK, LANES), 1)

    def run_blocks(blocks, own):
        k_b = [k_ref[n] for n in blocks]
        vt_b = [vt_ref[n] for n in blocks]
        sel_b = [None if own else sel_ref[pl.ds(n, 1), :] for n in blocks]
        bt_b = [None if own else slope2 * ((n - qb) * MOBA_BLOCK).astype(F32) for n in blocks]
        tasks = [(bi, c) for bi in range(len(blocks)) for c in range(n_chunks)]

        def score(t):
            bi, c = tasks[t]
            return lax.dot_general(k_b[bi], qs_ref[c * LANES:(c + 1) * LANES, :], _NT,
                                   preferred_element_type=F32)

        ahead = 2
        pending = {t: score(t) for t in range(min(ahead, len(tasks)))}
        for t, (bi, c) in enumerate(tasks):
            cs = slice(c * LANES, (c + 1) * LANES)
            u = pending.pop(t) - bias_ref[:, cs]
            if own:
                u = jnp.where(kk <= tt + (c * LANES) % MOBA_BLOCK, u, NEG)
                m_new = jnp.max(u, axis=0, keepdims=True)
                p = jnp.exp2(u - m_new)
                l_new = jnp.sum(p, axis=0, keepdims=True)
            else:
                picked = sel_b[bi][:, cs] > 0.0
                bt = bt_b[bi][:, cs]
                m_old = m_ref[:, cs]
                m_new = jnp.maximum(
                    m_old, jnp.where(picked, jnp.max(u, axis=0, keepdims=True) + bt, NEG))
                p = jnp.exp2(u - jnp.where(picked, m_new - bt, -NEG))
                alpha = jnp.exp2(m_old - m_new)
                l_new = alpha * l_ref[:, cs] + jnp.sum(p, axis=0, keepdims=True)
            m_ref[:, cs] = m_new
            l_ref[:, cs] = l_new
            if t + ahead < len(tasks):
                pending[t + ahead] = score(t + ahead)
            pv = jnp.dot(vt_b[bi], p.astype(BF16), preferred_element_type=F32)
            acc_ref[:, cs] = pv if own else alpha * acc_ref[:, cs] + pv

    run_blocks([qb], own=True)

    def body(i, carry):
        run_blocks([i * MOBA_UNROLL + j for j in range(MOBA_UNROLL)], own=False)
        return carry

    assert MOBA_UNROLL == 4
    n_full = qb // MOBA_UNROLL
    lax.fori_loop(0, n_full, body, 0)
    base = n_full * MOBA_UNROLL
    rem = qb - base

    @pl.when(rem >= 2)
    def _():
        run_blocks([base, base + 1], own=False)

    @pl.when(rem % 2 == 1)
    def _():
        run_blocks([base + (rem // 2) * 2], own=False)
    o_t = acc_ref[...] * (1.0 / l_ref[...])
    for r in range(MOBA_REP):
        o_ref[:, r * HEAD_DIM:(r + 1) * HEAD_DIM] = \
            o_t[:, r * MOBA_BLOCK:(r + 1) * MOBA_BLOCK].T.astype(BF16)


def _moba_attention(q, k_blocks, vt_blocks, k_mean, slope_lanes):
    s = q.shape[0]
    g, nb = k_blocks.shape[:2]
    rows = MOBA_REP * MOBA_BLOCK
    qw = MOBA_REP * HEAD_DIM
    return pl.pallas_call(
        _moba_kernel,
        out_shape=jax.ShapeDtypeStruct((s, g * qw), BF16),
        grid=(g, nb),
        in_specs=[pl.BlockSpec((MOBA_BLOCK, qw), lambda gi, qb: (qb, gi)),
                  pl.BlockSpec((None, nb, MOBA_BLOCK, HEAD_DIM), lambda gi, qb: (gi, 0, 0, 0)),
                  pl.BlockSpec((None, nb, HEAD_DIM, MOBA_BLOCK), lambda gi, qb: (gi, 0, 0, 0)),
                  pl.BlockSpec((None, nb, HEAD_DIM), lambda gi, qb: (gi, 0, 0)),
                  pl.BlockSpec((None, 1, rows), lambda gi, qb: (gi, 0, 0))],
        out_specs=pl.BlockSpec((MOBA_BLOCK, qw), lambda gi, qb: (qb, gi)),
        scratch_shapes=[pltpu.VMEM((rows, HEAD_DIM), BF16),
                        pltpu.VMEM((nb, rows), F32),
                        pltpu.VMEM((MOBA_BLOCK, rows), F32),
                        pltpu.VMEM((1, rows), F32),
                        pltpu.VMEM((1, rows), F32),
                        pltpu.VMEM((HEAD_DIM, rows), F32)],
        compiler_params=_params(("parallel", "arbitrary")),
        name="moba_attention",
    )(q, k_blocks, vt_blocks, k_mean, slope_lanes)


def _moe_kernel(blk_e_ref, total_ref, blk_first_ref, slot_tok_ref, slot_dst_ref,
                x_hbm, wg_ref, wu_ref, wd_ref, y_hbm,
                xbuf, obuf, wg_bf, wu_bf, wd_bf, gsem, ssem):
    b = pl.program_id(0)
    total = total_ref[0]
    slot = b % 2

    def gather_copy(blk, j, buf_slot):
        tok = slot_tok_ref[blk * MOE_BLOCK + j]
        return pltpu.make_async_copy(x_hbm.at[pl.ds(tok, 1)], xbuf.at[buf_slot, pl.ds(j, 1)],
                                     gsem.at[buf_slot])

    def scatter_copy(blk, j, buf_slot):
        dst = slot_dst_ref[blk * MOE_BLOCK + j]
        return pltpu.make_async_copy(obuf.at[buf_slot, pl.ds(j, 1)], y_hbm.at[pl.ds(dst, 1)],
                                     ssem.at[buf_slot])

    def wait_gather(buf_slot):
        pltpu.make_async_copy(x_hbm.at[pl.ds(0, MOE_BLOCK)], xbuf.at[buf_slot],
                              gsem.at[buf_slot]).wait()

    def wait_scatter(buf_slot):
        pltpu.make_async_copy(obuf.at[buf_slot], y_hbm.at[pl.ds(0, MOE_BLOCK)],
                              ssem.at[buf_slot]).wait()

    @pl.when(b == 0)
    def _():
        n_real = y_hbm.shape[0] - 2 * MOE_BLOCK
        obuf[0] = jnp.zeros(obuf.shape[1:], U32)
        for i in range(2):
            dump = y_hbm.at[pl.ds(n_real + i * MOE_BLOCK, MOE_BLOCK)]
            pltpu.make_async_copy(obuf.at[0], dump, ssem.at[i]).start()
        for i in range(2):
            wait_scatter(i)
        for j in range(MOE_BLOCK):
            gather_copy(0, j, 0).start()

    @pl.when(b < total)
    def _():
        @pl.when(blk_first_ref[b] == 1)
        def _():
            wg_bf[...] = wg_ref[...].astype(BF16)
            wu_bf[...] = wu_ref[...].astype(BF16)
            wd_bf[...] = wd_ref[...].astype(BF16)

        wait_gather(slot)
        for j in range(MOE_BLOCK):
            gather_copy(b + 1, j, 1 - slot).start(priority=j % 2)
        h_lo, h_hi = _unpack_halves(xbuf[slot])
        hb = jnp.concatenate([h_lo.astype(BF16), h_hi.astype(BF16)], axis=1)
        gt = jnp.dot(hb, wg_bf[...], preferred_element_type=F32)
        up = jnp.dot(hb, wu_bf[...], preferred_element_type=F32)
        hid = gt * (1.0 / (1.0 + jnp.exp(-gt))) * up
        obuf[slot] = _pack_halves(jnp.dot(hid.astype(BF16), wd_bf[...],
                                          preferred_element_type=F32))
        for j in range(MOE_BLOCK):
            scatter_copy(b, j, slot).start(priority=j % 2)

        @pl.when(b >= 1)
        def _():
            wait_scatter(1 - slot)

        @pl.when(b == total - 1)
        def _():
            wait_scatter(slot)
            wait_gather(1 - slot)


def _moe_dispatch(ids, n_experts):
    s, k = ids.shape
    a = s * k
    n_blocks = -(-a // MOE_BLOCK) + n_experts
    e_flat = ids.reshape(a)
    order = jnp.argsort(e_flat, stable=True).astype(jnp.int32)
    experts = jnp.arange(n_experts, dtype=jnp.int32)
    counts = jnp.sum((e_flat[:, None] == experts[None, :]).astype(jnp.int32), axis=0)
    starts = jnp.cumsum(counts) - counts
    nblk = (counts + MOE_BLOCK - 1) // MOE_BLOCK
    blk_end = jnp.cumsum(nblk)
    blk_start = blk_end - nblk
    total = blk_end[-1]
    bidx = jnp.arange(n_blocks, dtype=jnp.int32)
    used = bidx < total
    b_eff = jnp.minimum(bidx, total - 1)
    blk_e = jnp.sum((blk_end[None, :] <= b_eff[:, None]).astype(jnp.int32), axis=1)
    blk_e = jnp.minimum(blk_e, n_experts - 1)
    rank0 = (bidx - blk_start[blk_e]) * MOE_BLOCK
    blk_first = (used & (rank0 == 0)).astype(jnp.int32)
    lane = jnp.arange(MOE_BLOCK, dtype=jnp.int32)[None, :]
    rank = rank0[:, None] + lane
    valid = used[:, None] & (rank < counts[blk_e][:, None])
    src = jnp.clip(starts[blk_e][:, None] + rank, 0, a - 1)
    asg = order[src]
    slot_tok = jnp.where(valid, asg // k, 0)
    slot_tok = jnp.concatenate([slot_tok, jnp.zeros((1, MOE_BLOCK), jnp.int32)]).reshape(-1)
    dump = a + (bidx[:, None] % 2) * MOE_BLOCK + lane
    slot_dst = jnp.where(valid, (asg % k) * s + asg // k, dump).reshape(-1)
    return blk_e, total.reshape(1), blk_first, slot_tok, slot_dst, n_blocks


def _moe_experts(hfp, ids, layer, w_gate, w_up, w_down):
    s, dp = hfp.shape
    d = 2 * dp
    n_experts, _, f = w_gate.shape[1:]
    k = ids.shape[1]
    blk_e, total, blk_first, slot_tok, slot_dst, n_blocks = _moe_dispatch(ids, n_experts)
    w_idx = lambda b, be, *_: (layer, be[b], 0, 0)
    return pl.pallas_call(
        _moe_kernel,
        out_shape=jax.ShapeDtypeStruct((s * k + 2 * MOE_BLOCK, dp), U32),
        grid_spec=pltpu.PrefetchScalarGridSpec(
            num_scalar_prefetch=5,
            grid=(n_blocks,),
            in_specs=[pl.BlockSpec(memory_space=pl.ANY),
                      pl.BlockSpec((None, None, d, f), w_idx),
                      pl.BlockSpec((None, None, d, f), w_idx),
                      pl.BlockSpec((None, None, f, d), w_idx)],
            out_specs=pl.BlockSpec(memory_space=pl.ANY),
            scratch_shapes=[pltpu.VMEM((2, MOE_BLOCK, dp), U32),
                            pltpu.VMEM((2, MOE_BLOCK, dp), U32),
                            pltpu.VMEM((d, f), BF16),
                            pltpu.VMEM((d, f), BF16),
                            pltpu.VMEM((f, d), BF16),
                            pltpu.SemaphoreType.DMA((2,)),
                            pltpu.SemaphoreType.DMA((2,))]),
        compiler_params=_params(("arbitrary",)),
        name="moe_experts",
    )(blk_e, total, blk_first, slot_tok, slot_dst, hfp, w_gate, w_up, w_down)


def _vec_pack(d, **rows):
    out = jnp.zeros((8, d), F32)
    for name, v in rows.items():
        out = out.at[name_to_row[name]].set(v)
    return out


name_to_row = {"gate": _V_GATE, "gain": _V_GAIN, "bias": _V_BIAS, "scale_a": _V_SCALE_A,
               "shift_a": _V_SHIFT_A, "scale_b": _V_SCALE_B, "shift_b": _V_SHIFT_B}


def kernel(x, c, ada_down, ada_up, ada_bias, ln_gain, ln_bias, a_w_qkv, a_w_o, kv_ada_down, kv_ada_up, kv_ada_bias, kv_w, b_w_q, b_w_o, moe_w_group, moe_b_group, moe_w_expert, moe_b_expert, moe_w_gate, moe_w_up, moe_w_down):
    batch, s, d = x.shape
    assert batch == 1
    depth = ada_down.shape[0]
    n_a = a_w_qkv.shape[0]
    alpha = (2 * depth) ** 0.25
    a_heads = a_w_o.shape[1] // HEAD_DIM
    a_width = a_heads * HEAD_DIM
    n_groups = moe_w_group.shape[-1]
    n_per_group = moe_w_expert.shape[-1]
    kv_heads = kv_w.shape[1] // (2 * HEAD_DIM)
    kv_width = kv_heads * HEAD_DIM
    b_heads = kv_heads * MOBA_REP
    n_blk = s // MOBA_BLOCK

    mods = _modulation(c, ada_down, ada_up, ada_bias).reshape(depth, 6, d)
    kv_mod = _modulation(c, kv_ada_down[None], kv_ada_up[None], kv_ada_bias[None]).reshape(2, d)

    x2 = x.reshape(s, d)
    hs = _prologue(x2, _vec_pack(d, scale_a=mods[0, 1], shift_a=mods[0, 0]))

    k_blocks = vt_blocks = k_mean = slope_lanes = None
    for l in range(depth):
        shift1, scale1, gate1, shift2, scale2, gate2 = [mods[l, i] for i in range(6)]
        if l < n_a:
            outs, lses = [], []
            for g, (_, dil) in enumerate(A_GROUPS):
                qkv = _matmul(hs, a_w_qkv, l, BF16, n_off=g * 3 * a_width, n_out=3 * a_width,
                              dil=dil)
                o_g, lse_g = _dilated_attention(qkv, dil, a_heads)
                outs.append(o_g)
                lses.append(lse_g)
            merged = _merge_groups(outs, lses, a_heads)
            mix = _matmul(merged, a_w_o, l, BF16)
        else:
            j = l - n_a
            q = _matmul(hs, b_w_q, j, F32)
            o = _moba_attention(q, k_blocks, vt_blocks, k_mean, slope_lanes)
            mix = _matmul(o, b_w_o, j, BF16)

        wr = jnp.concatenate(
            [moe_w_group[l], jnp.moveaxis(moe_w_expert[l], 0, 1).reshape(d, n_groups * n_per_group),
             jnp.zeros((d, LANES - n_groups * (1 + n_per_group)), F32)], axis=1)
        br = jnp.concatenate(
            [moe_b_group[l], moe_b_expert[l].reshape(-1),
             jnp.zeros((LANES - n_groups * (1 + n_per_group),), F32)]).reshape(1, LANES)
        vec1 = _vec_pack(d, gate=gate1, gain=ln_gain[l, 0], bias=ln_bias[l, 0],
                         scale_a=scale2, shift_a=shift2)
        x2, route, hfp = _ln_router(x2, mix, vec1, wr, br, alpha, n_groups, n_per_group)
        ids = route[:, :MOE_TOPK].astype(jnp.int32)
        y2 = _moe_experts(hfp, ids, l, moe_w_gate, moe_w_up, moe_w_down)

        last = l == depth - 1
        with_kv = l == n_a - 1 and n_a < depth
        rows = dict(gate=gate2, gain=ln_gain[l, 1], bias=ln_bias[l, 1])
        if not last:
            rows.update(scale_a=mods[l + 1, 1], shift_a=mods[l + 1, 0])
        if with_kv:
            rows.update(scale_b=kv_mod[1], shift_b=kv_mod[0])
        res = _ln_combine(x2, y2, route, _vec_pack(d, **rows), alpha, not last, with_kv)
        x2 = res[0]
        if with_kv:
            kv = _matmul(res[1], kv_w[None], 0, F32)
            k_f32 = kv[:, :kv_width]
            k_mean = _block_means(k_f32).reshape(n_blk, kv_heads, HEAD_DIM).transpose(1, 0, 2)
            k_blocks = k_f32.astype(BF16).reshape(n_blk, MOBA_BLOCK, kv_heads, HEAD_DIM) \
                .transpose(2, 0, 1, 3)
            vt_blocks = kv[:, kv_width:].astype(BF16).reshape(n_blk, MOBA_BLOCK, kv_heads, HEAD_DIM) \
                .transpose(2, 0, 3, 1)
            slopes = jnp.exp2(-8.0 * jnp.arange(1, b_heads + 1, dtype=F32) / b_heads)
            slope_lanes = jnp.repeat(slopes.reshape(kv_heads, MOBA_REP), MOBA_BLOCK, axis=1) \
                .reshape(kv_heads, 1, MOBA_REP * MOBA_BLOCK)
        if not last:
            hs = res[-1]
    return x2.reshape(batch, s, d)
```

```python
import functools
import math

import numpy as np
import jax
import jax.numpy as jnp
from jax import lax
from jax.experimental import pallas as pl
from jax.experimental.pallas import tpu as pltpu

F32 = jnp.float32
BF16 = jnp.bfloat16
U32 = jnp.uint32
HIGHEST = lax.Precision.HIGHEST

HEAD_DIM = 128
A_GROUPS = ((128, 1), (512, 4), (2048, 16))
A_BLOCK = 128
MOBA_BLOCK = 256
MOBA_TOPK = 3
MOBA_REP = 4
MOBA_UNROLL = 4
MOE_TOPK = 2
MOE_BLOCK = 128
LN_EPS = 1e-5
NEG = -1e30
LOG2E = math.log2(math.e)
LANES = 128
VMEM_LIMIT = 56 * 1024 * 1024

_NT = (((1,), (1,)), ((), ()))


def _params(sem, vmem=VMEM_LIMIT):
    return pltpu.CompilerParams(dimension_semantics=sem, vmem_limit_bytes=vmem)


def _mod_kernel(c_ref, wd_ref, wu_ref, b_ref, o_ref, t_ref):
    @pl.when(pl.program_id(1) == 0)
    def _():
        c = c_ref[...]
        sc = c * (1.0 / (1.0 + jnp.exp(-c)))
        t_ref[...] = jnp.dot(sc, wd_ref[...], preferred_element_type=F32, precision=HIGHEST)

    o_ref[...] = jnp.dot(t_ref[...], wu_ref[...], preferred_element_type=F32,
                         precision=HIGHEST) + b_ref[...]


def _modulation(c, w_down, w_up, b_up):
    n_l, d, r = w_down.shape
    n = w_up.shape[-1]
    tn = n // pl.cdiv(n, 6144)
    c8 = jnp.broadcast_to(c, (8, d))
    out = pl.pallas_call(
        _mod_kernel,
        out_shape=jax.ShapeDtypeStruct((n_l, 8, n), F32),
        grid=(n_l, n // tn),
        in_specs=[
            pl.BlockSpec((8, d), lambda l, j: (0, 0)),
            pl.BlockSpec((None, d, r), lambda l, j: (l, 0, 0)),
            pl.BlockSpec((None, r, tn), lambda l, j: (l, 0, j)),
            pl.BlockSpec((None, 1, tn), lambda l, j: (l, 0, j)),
        ],
        out_specs=pl.BlockSpec((None, 8, tn), lambda l, j: (l, 0, j)),
        scratch_shapes=[pltpu.VMEM((8, r), F32)],
        compiler_params=_params(("arbitrary", "arbitrary")),
        name="modulation",
    )(c8, w_down, w_up, b_up.reshape(n_l, 1, n))
    return out[:, 0, :]


def _mm_kernel(a_ref, b_ref, o_ref, b_bf_ref):
    @pl.when(pl.program_id(1) == 0)
    def _():
        b_bf_ref[...] = b_ref[...].astype(BF16)

    o_ref[...] = jnp.dot(a_ref[...], b_bf_ref[...],
                         preferred_element_type=F32).astype(o_ref.dtype)


def _matmul_blocks(out_dtype):
    return (512, 1024) if jnp.dtype(out_dtype).itemsize == 2 else (1024, 512)


def _matmul(a, b, layer, out_dtype, n_off=0, n_out=None, dil=1):
    m, k = a.shape
    tm, tn = _matmul_blocks(out_dtype)
    n_out = b.shape[2] if n_out is None else n_out
    tm = min(tm, m // dil)
    tn = min(tn, n_out)
    off = n_off // tn
    bps = m // dil // tm
    return pl.pallas_call(
        _mm_kernel,
        out_shape=jax.ShapeDtypeStruct((m, n_out), out_dtype),
        grid=(n_out // tn, m // tm),
        in_specs=[
            pl.BlockSpec((tm, k), lambda j, i: (i % bps, i // bps)),
            pl.BlockSpec((None, k, tn), lambda j, i: (layer, 0, j + off)),
        ],
        out_specs=pl.BlockSpec((tm, tn), lambda j, i: (i, j)),
        scratch_shapes=[pltpu.VMEM((k, tn), BF16)],
        compiler_params=_params(("parallel", "arbitrary")),
        name="matmul",
    )(a.reshape(m // dil, dil * k), b)


def _pack_halves(v):
    n = v.shape[1] // 2
    hi = lax.bitcast_convert_type(v[:, :n].astype(BF16).astype(F32), U32)
    lo = lax.bitcast_convert_type(v[:, n:].astype(BF16).astype(F32), U32)
    return (hi & jnp.uint32(0xFFFF0000)) | (lo >> 16)


def _unpack_halves(p):
    return (lax.bitcast_convert_type(p & jnp.uint32(0xFFFF0000), F32),
            lax.bitcast_convert_type(p << 16, F32))


def _layer_norm(v, gain, bias):
    mu = jnp.mean(v, axis=-1, keepdims=True)
    vc = v - mu
    var = jnp.mean(vc * vc, axis=-1, keepdims=True)
    return vc * lax.rsqrt(var + LN_EPS) * gain + bias


_V_GATE, _V_GAIN, _V_BIAS, _V_SCALE_A, _V_SHIFT_A, _V_SCALE_B, _V_SHIFT_B = range(7)


def _prologue_kernel(x_ref, vec_ref, o_ref):
    h = x_ref[...] * (1.0 + vec_ref[_V_SCALE_A:_V_SCALE_A + 1, :]) \
        + vec_ref[_V_SHIFT_A:_V_SHIFT_A + 1, :]
    o_ref[...] = h.astype(BF16)


def _prologue(x2, vecs, tm=256):
    s, d = x2.shape
    return pl.pallas_call(
        _prologue_kernel,
        out_shape=jax.ShapeDtypeStruct((s, d), BF16),
        grid=(s // tm,),
        in_specs=[pl.BlockSpec((tm, d), lambda i: (i, 0)),
                  pl.BlockSpec((8, d), lambda i: (0, 0))],
        out_specs=pl.BlockSpec((tm, d), lambda i: (i, 0)),
        compiler_params=_params(("parallel",)),
        name="prologue",
    )(x2, vecs)


def _ln_router_kernel(alpha, n_groups, n_per_group,
                      x_ref, mix_ref, vec_ref, wrh_ref, wrl_ref, br_ref, xo_ref, route_ref, hfp_ref):
    gate = vec_ref[_V_GATE:_V_GATE + 1, :]
    x1 = alpha * x_ref[...] + (1.0 + gate) * mix_ref[...].astype(F32)
    xn = _layer_norm(x1, vec_ref[_V_GAIN:_V_GAIN + 1, :], vec_ref[_V_BIAS:_V_BIAS + 1, :])
    xo_ref[...] = xn
    hf = xn * (1.0 + vec_ref[_V_SCALE_A:_V_SCALE_A + 1, :]) + vec_ref[_V_SHIFT_A:_V_SHIFT_A + 1, :]
    hfp_ref[...] = _pack_halves(hf)
    hf_hi = hf.astype(BF16)
    hf_lo = (hf - hf_hi.astype(F32)).astype(BF16)
    logits = jnp.dot(hf_hi, wrh_ref[...], preferred_element_type=F32) \
        + jnp.dot(hf_hi, wrl_ref[...], preferred_element_type=F32) \
        + jnp.dot(hf_lo, wrh_ref[...], preferred_element_type=F32) + br_ref[...]
    lane = lax.broadcasted_iota(jnp.int32, logits.shape, 1).astype(F32)
    ninf = -jnp.inf
    big = float(LANES)
    is_g = lane < n_groups
    gl = jnp.where(is_g, logits, ninf)
    gmax = jnp.max(gl, axis=-1, keepdims=True)
    gsel = jnp.min(jnp.where(gl == gmax, lane, big), axis=-1, keepdims=True)
    p_group = 1.0 / jnp.sum(jnp.where(is_g, jnp.exp(logits - gmax), 0.0), axis=-1, keepdims=True)
    lo = n_groups + gsel * n_per_group
    el = jnp.where((lane >= lo) & (lane < lo + n_per_group), logits, ninf)
    m1 = jnp.max(el, axis=-1, keepdims=True)
    i1 = jnp.min(jnp.where(el == m1, lane, big), axis=-1, keepdims=True)
    el2 = jnp.where(lane == i1, ninf, el)
    m2 = jnp.max(el2, axis=-1, keepdims=True)
    i2 = jnp.min(jnp.where(el2 == m2, lane, big), axis=-1, keepdims=True)
    e21 = jnp.exp(m2 - m1)
    inv = 1.0 / (1.0 + e21)
    w1 = p_group * inv
    w2 = p_group * (e21 * inv)
    route_ref[...] = jnp.where(lane == 0.0, i1 - n_groups,
                     jnp.where(lane == 1.0, i2 - n_groups,
                     jnp.where(lane == 2.0, w1,
                     jnp.where(lane == 3.0, w2, 0.0))))


def _ln_router(x2, mix, vecs, wr, br, alpha, n_groups, n_per_group, tm=256):
    s, d = x2.shape
    wr_top = lax.bitcast_convert_type(
        lax.bitcast_convert_type(wr, U32) & jnp.uint32(0xFFFF0000), F32)
    wr_hi = wr_top.astype(BF16)
    wr_lo = (wr - wr_top).astype(BF16)
    return pl.pallas_call(
        functools.partial(_ln_router_kernel, alpha, n_groups, n_per_group),
        out_shape=[jax.ShapeDtypeStruct((s, d), F32),
                   jax.ShapeDtypeStruct((s, LANES), F32),
                   jax.ShapeDtypeStruct((s, d // 2), U32)],
        grid=(s // tm,),
        in_specs=[pl.BlockSpec((tm, d), lambda i: (i, 0)),
                  pl.BlockSpec((tm, d), lambda i: (i, 0)),
                  pl.BlockSpec((8, d), lambda i: (0, 0)),
                  pl.BlockSpec((d, LANES), lambda i: (0, 0)),
                  pl.BlockSpec((d, LANES), lambda i: (0, 0)),
                  pl.BlockSpec((1, LANES), lambda i: (0, 0))],
        out_specs=[pl.BlockSpec((tm, d), lambda i: (i, 0)),
                   pl.BlockSpec((tm, LANES), lambda i: (i, 0)),
                   pl.BlockSpec((tm, d // 2), lambda i: (i, 0))],
        compiler_params=_params(("parallel",)),
        name="ln_router",
    )(x2, mix, vecs, wr_hi, wr_lo, br)


def _ln_combine_kernel(alpha, with_next, with_kv, x_ref, y0_ref, y1_ref, route_ref, vec_ref, *refs):
    w1, w2 = route_ref[:, 2:3], route_ref[:, 3:4]
    a0, b0 = _unpack_halves(y0_ref[...])
    a1, b1 = _unpack_halves(y1_ref[...])
    ffn = jnp.concatenate([a0 * w1 + a1 * w2, b0 * w1 + b1 * w2], axis=1)
    gate = vec_ref[_V_GATE:_V_GATE + 1, :]
    x1 = alpha * x_ref[...] + (1.0 + gate) * ffn
    xn = _layer_norm(x1, vec_ref[_V_GAIN:_V_GAIN + 1, :], vec_ref[_V_BIAS:_V_BIAS + 1, :])
    refs[0][...] = xn
    k = 1
    if with_kv:
        hk = xn * (1.0 + vec_ref[_V_SCALE_B:_V_SCALE_B + 1, :]) + vec_ref[_V_SHIFT_B:_V_SHIFT_B + 1, :]
        refs[k][...] = hk.astype(BF16)
        k += 1
    if with_next:
        h = xn * (1.0 + vec_ref[_V_SCALE_A:_V_SCALE_A + 1, :]) + vec_ref[_V_SHIFT_A:_V_SHIFT_A + 1, :]
        refs[k][...] = h.astype(BF16)


def _ln_combine(x2, y2, route, vecs, alpha, with_next, with_kv, tm=256):
    s, d = x2.shape
    row = lambda i: (i, 0)
    n_bf = int(with_kv) + int(with_next)
    shapes = [jax.ShapeDtypeStruct((s, d), F32)] + [jax.ShapeDtypeStruct((s, d), BF16)] * n_bf
    specs = [pl.BlockSpec((tm, d), row)] * (1 + n_bf)
    return pl.pallas_call(
        functools.partial(_ln_combine_kernel, alpha, with_next, with_kv),
        out_shape=shapes,
        grid=(s // tm,),
        in_specs=[pl.BlockSpec((tm, d), row),
                  pl.BlockSpec((tm, d // 2), row),
                  pl.BlockSpec((tm, d // 2), lambda i: (i + s // tm, 0)),
                  pl.BlockSpec((tm, LANES), row),
                  pl.BlockSpec((8, d), lambda i: (0, 0))],
        out_specs=specs,
        compiler_params=_params(("parallel",)),
        name="ln_combine",
    )(x2, y2, y2, route, vecs)


def _dilated_kernel(dil, blocks_per_stream, slopes, q_ref, kp_ref, kc_ref, vp_ref, vc_ref,
                    o_ref, lse_ref):
    nb = pl.program_id(0) % blocks_per_stream
    has_prev = nb > 0
    qi = lax.broadcasted_iota(jnp.int32, (A_BLOCK, A_BLOCK), 0)
    kj = lax.broadcasted_iota(jnp.int32, (A_BLOCK, A_BLOCK), 1)
    dist_prev = jnp.where((kj >= qi) & has_prev, ((qi + A_BLOCK - kj) * dil).astype(F32), -NEG)
    dist_cur = jnp.where(kj <= qi, ((qi - kj) * dil).astype(F32), -NEG)
    sqrt_e = math.sqrt(HEAD_DIM)
    lane = lax.broadcasted_iota(jnp.int32, (A_BLOCK, LANES), 1)
    ones = jnp.ones((A_BLOCK, LANES), BF16)
    n_heads = len(slopes)

    def scores(h):
        hs = slice(h * HEAD_DIM, (h + 1) * HEAD_DIM)
        q = q_ref[:, hs]
        return (lax.dot_general(q, kp_ref[:, hs], _NT, preferred_element_type=F32),
                lax.dot_general(q, kc_ref[:, hs], _NT, preferred_element_type=F32))

    ahead = 6
    pending = {h: scores(h) for h in range(min(ahead, n_heads))}
    lse_all = jnp.zeros((A_BLOCK, LANES), F32)
    for h, slope in enumerate(slopes):
        hs = slice(h * HEAD_DIM, (h + 1) * HEAD_DIM)
        s_p, s_c = pending.pop(h)
        t_p = s_p - (slope * sqrt_e) * dist_prev
        t_c = s_c - (slope * sqrt_e) * dist_cur
        t_max = jnp.max(jnp.maximum(t_p, t_c), axis=-1, keepdims=True)
        p_p = jnp.exp2((t_p - t_max) * (LOG2E / sqrt_e)).astype(BF16)
        p_c = jnp.exp2((t_c - t_max) * (LOG2E / sqrt_e)).astype(BF16)
        if h + ahead < n_heads:
            pending[h + ahead] = scores(h + ahead)
        o = jnp.dot(p_p, vp_ref[:, hs], preferred_element_type=F32) \
            + jnp.dot(p_c, vc_ref[:, hs], preferred_element_type=F32)
        l = jnp.dot(p_p, ones, preferred_element_type=F32) \
            + jnp.dot(p_c, ones, preferred_element_type=F32)
        o_ref[:, hs] = o * (1.0 / l)
        lse_all = jnp.where(lane == h, t_max * (1.0 / sqrt_e) + jnp.log(l), lse_all)
    lse_ref[...] = lse_all


def _dilated_attention(qkv, dil, n_heads):
    s = qkv.shape[0]
    w = n_heads * HEAD_DIM
    n = s // dil
    bps = n // A_BLOCK
    slopes = [float(v) for v in np.exp2(-8.0 * np.arange(1, n_heads + 1, dtype=np.float32) / n_heads)]
    prev = lambda b: (jnp.maximum(b - 1, 0), 1)
    o, lse = pl.pallas_call(
        functools.partial(_dilated_kernel, dil, bps, slopes),
        out_shape=[jax.ShapeDtypeStruct((n, dil * w), F32),
                   jax.ShapeDtypeStruct((n, dil * LANES), F32)],
        grid=(s // A_BLOCK,),
        in_specs=[pl.BlockSpec((A_BLOCK, w), lambda b: (b, 0)),
                  pl.BlockSpec((A_BLOCK, w), prev),
                  pl.BlockSpec((A_BLOCK, w), lambda b: (b, 1)),
                  pl.BlockSpec((A_BLOCK, w), lambda b: (jnp.maximum(b - 1, 0), 2)),
                  pl.BlockSpec((A_BLOCK, w), lambda b: (b, 2))],
        out_specs=[pl.BlockSpec((A_BLOCK, w), lambda b: (b % bps, b // bps)),
                   pl.BlockSpec((A_BLOCK, LANES), lambda b: (b % bps, b // bps))],
        compiler_params=_params(("parallel",)),
        name="dilated_attention",
    )(qkv, qkv, qkv, qkv, qkv)
    return o, lse


def _merge_kernel(n_heads, *refs):
    n_g = len(A_GROUPS)
    o_refs, l_refs = refs[:n_g], refs[n_g:2 * n_g]
    out_ref = refs[2 * n_g]
    stages = iter(refs[2 * n_g + 1:])
    tm = out_ref.shape[0]
    o_nat, l_nat = [], []
    for (_, dil), o_ref, l_ref in zip(A_GROUPS, o_refs, l_refs):
        if dil == 1:
            o_nat.append(lambda h, o_ref=o_ref: o_ref[:, h * HEAD_DIM:(h + 1) * HEAD_DIM])
            l_nat.append(l_ref[...])
            continue
        stage = next(stages)
        rows = tm // dil
        w = o_ref.shape[1] // dil
        for r in range(dil):
            for h in range(n_heads):
                stage[h, pl.ds(r, rows, stride=dil), :] = \
                    o_ref[:, r * w + h * HEAD_DIM:r * w + (h + 1) * HEAD_DIM]
            stage[n_heads, pl.ds(r, rows, stride=dil), :] = l_ref[:, r * LANES:(r + 1) * LANES]
        o_nat.append(lambda h, stage=stage: stage[h])
        l_nat.append(stage[n_heads])
    m = functools.reduce(jnp.maximum, l_nat)
    es = [jnp.exp(l - m) for l in l_nat]
    inv = 1.0 / functools.reduce(lambda a, b: a + b, es)
    ws = [e * inv for e in es]
    for h in range(n_heads):
        acc = functools.reduce(lambda a, b: a + b,
                               [o(h) * w_g[:, h:h + 1] for o, w_g in zip(o_nat, ws)])
        out_ref[:, h * HEAD_DIM:(h + 1) * HEAD_DIM] = acc.astype(BF16)


def _merge_groups(os_, lses, n_heads, tm=256):
    w = n_heads * HEAD_DIM
    s = os_[0].shape[0] * A_GROUPS[0][1]
    row = lambda i: (i, 0)
    dils = [dil for _, dil in A_GROUPS]
    return pl.pallas_call(
        functools.partial(_merge_kernel, n_heads),
        out_shape=jax.ShapeDtypeStruct((s, w), BF16),
        grid=(s // tm,),
        in_specs=[pl.BlockSpec((tm // dil, dil * w), row) for dil in dils]
        + [pl.BlockSpec((tm // dil, dil * LANES), row) for dil in dils],
        out_specs=pl.BlockSpec((tm, w), row),
        scratch_shapes=[pltpu.VMEM((n_heads + 1, tm, LANES), F32) for dil in dils if dil > 1],
        compiler_params=_params(("parallel",)),
        name="merge_groups",
    )(*os_, *lses)


def _kmean_kernel(k_ref, o_ref):
    o_ref[...] = jnp.mean(k_ref[...], axis=0, keepdims=True)


def _block_means(k):
    s, kw = k.shape
    nb = s // MOBA_BLOCK
    return pl.pallas_call(
        _kmean_kernel,
        out_shape=jax.ShapeDtypeStruct((nb, 1, kw), F32),
        grid=(nb,),
        in_specs=[pl.BlockSpec((MOBA_BLOCK, kw), lambda i: (i, 0))],
        out_specs=pl.BlockSpec((None, 1, kw), lambda i: (i, 0, 0)),
        compiler_params=_params(("parallel",)),
        name="block_means",
    )(k)


def _moba_kernel(q_ref, k_ref, vt_ref, kmean_ref, slope_ref, o_ref,
                 qs_ref, sel_ref, bias_ref, m_ref, l_ref, acc_ref):
    qb = pl.program_id(1)
    n_blk = kmean_ref.shape[0]
    rows = MOBA_REP * MOBA_BLOCK
    n_chunks = rows // LANES
    slope2 = slope_ref[...] * LOG2E

    @pl.when(qb == 0)
    def _():
        lane_t = (lax.broadcasted_iota(jnp.int32, (1, rows), 1) % MOBA_BLOCK).astype(F32)
        sub = lax.broadcasted_iota(jnp.int32, (MOBA_BLOCK, 1), 0).astype(F32)
        bias_ref[...] = slope2 * (lane_t - sub)

    q = q_ref[...]
    q4 = jnp.concatenate([q[:, r * HEAD_DIM:(r + 1) * HEAD_DIM] for r in range(MOBA_REP)], axis=0)
    gate = lax.dot_general(kmean_ref[...], q4, _NT, preferred_element_type=F32, precision=HIGHEST)
    blk = lax.broadcasted_iota(jnp.int32, (n_blk, rows), 0).astype(F32)
    ninf = -jnp.inf
    gate = jnp.where(blk < qb.astype(F32), gate, ninf)
    sel = jnp.zeros((n_blk, rows), F32)
    for _ in range(MOBA_TOPK):
        top = jnp.max(gate, axis=0, keepdims=True)
        idx = jnp.min(jnp.where(gate == top, blk, float(n_blk)), axis=0, keepdims=True)
        hit = (blk == idx) & (top > ninf)
        sel = jnp.where(hit, 1.0, sel)
        gate = jnp.where(hit, ninf, gate)
    sel_ref[...] = sel
    qs_ref[...] = (q4 * (LOG2E / math.sqrt(HEAD_DIM))).astype(BF16)

    kk = lax.broadcasted_iota(jnp.int32, (MOBA_BLOCK, LANES), 0)
    tt = lax.broadcasted_iota(jnp.int32, (MOBA_BLOCK, LANES), 1)

    def run_blocks(blocks, own):
        k_b = [k_ref[n] for n in blocks]
        vt_b = [vt_ref[n] for n in blocks]
        sel_b = [None if own else sel_ref[pl.ds(n, 1), :] for n in blocks]
        bt_b = [None if own else slope2 * ((n - qb) * MOBA_BLOCK).astype(F32) for n in blocks]
        tasks = [(bi, c) for bi in range(len(blocks)) for c in range(n_chunks)]

        def score(t):
            bi, c = tasks[t]
            return lax.dot_general(k_b[bi], qs_ref[c * LANES:(c + 1) * LANES, :], _NT,
                                   preferred_element_type=F32)

        ahead = 2
        pending = {t: score(t) for t in range(min(ahead, len(tasks)))}
        for t, (bi, c) in enumerate(tasks):
            cs = slice(c * LANES, (c + 1) * LANES)
            u = pending.pop(t) - bias_ref[:, cs]
            if own:
                u = jnp.where(kk <= tt + (c * LANES) % MOBA_BLOCK, u, NEG)
                m_new = jnp.max(u, axis=0, keepdims=True)
                p = jnp.exp2(u - m_new)
                l_new = jnp.sum(p, axis=0, keepdims=True)
            else:
                picked = sel_b[bi][:, cs] > 0.0
                bt = bt_b[bi][:, cs]
                m_old = m_ref[:, cs]
                m_new = jnp.maximum(
                    m_old, jnp.where(picked, jnp.max(u, axis=0, keepdims=True) + bt, NEG))
                p = jnp.exp2(u - jnp.where(picked, m_new - bt, -NEG))
                alpha = jnp.exp2(m_old - m_new)
                l_new = alpha * l_ref[:, cs] + jnp.sum(p, axis=0, keepdims=True)
            m_ref[:, cs] = m_new
            l_ref[:, cs] = l_new
            if t + ahead < len(tasks):
                pending[t + ahead] = score(t + ahead)
            pv = jnp.dot(vt_b[bi], p.astype(BF16), preferred_element_type=F32)
            acc_ref[:, cs] = pv if own else alpha * acc_ref[:, cs] + pv

    run_blocks([qb], own=True)

    def body(i, carry):
        run_blocks([i * MOBA_UNROLL + j for j in range(MOBA_UNROLL)], own=False)
        return carry

    assert MOBA_UNROLL == 4
    n_full = qb // MOBA_UNROLL
    lax.fori_loop(0, n_full, body, 0)
    base = n_full * MOBA_UNROLL
    rem = qb - base

    @pl.when(rem >= 2)
    def _():
        run_blocks([base, base + 1], own=False)

    @pl.when(rem % 2 == 1)
    def _():
        run_blocks([base + (rem // 2) * 2], own=False)
    o_t = acc_ref[...] * (1.0 / l_ref[...])
    for r in range(MOBA_REP):
        o_ref[:, r * HEAD_DIM:(r + 1) * HEAD_DIM] = \
            o_t[:, r * MOBA_BLOCK:(r + 1) * MOBA_BLOCK].T.astype(BF16)


def _moba_attention(q, k_blocks, vt_blocks, k_mean, slope_lanes):
    s = q.shape[0]
    g, nb = k_blocks.shape[:2]
    rows = MOBA_REP * MOBA_BLOCK
    qw = MOBA_REP * HEAD_DIM
    return pl.pallas_call(
        _moba_kernel,
        out_shape=jax.ShapeDtypeStruct((s, g * qw), BF16),
        grid=(g, nb),
        in_specs=[pl.BlockSpec((MOBA_BLOCK, qw), lambda gi, qb: (qb, gi)),
                  pl.BlockSpec((None, nb, MOBA_BLOCK, HEAD_DIM), lambda gi, qb: (gi, 0, 0, 0)),
                  pl.BlockSpec((None, nb, HEAD_DIM, MOBA_BLOCK), lambda gi, qb: (gi, 0, 0, 0)),
                  pl.BlockSpec((None, nb, HEAD_DIM), lambda gi, qb: (gi, 0, 0)),
                  pl.BlockSpec((None, 1, rows), lambda gi, qb: (gi, 0, 0))],
        out_specs=pl.BlockSpec((MOBA_BLOCK, qw), lambda gi, qb: (qb, gi)),
        scratch_shapes=[pltpu.VMEM((rows, HEAD_DIM), BF16),
                        pltpu.VMEM((nb, rows), F32),
                        pltpu.VMEM((MOBA_BLOCK, rows), F32),
                        pltpu.VMEM((1, rows), F32),
                        pltpu.VMEM((1, rows), F32),
                        pltpu.VMEM((HEAD_DIM, rows), F32)],
        compiler_params=_params(("parallel", "arbitrary")),
        name="moba_attention",
    )(q, k_blocks, vt_blocks, k_mean, slope_lanes)


def _moe_kernel(blk_e_ref, total_ref, blk_first_ref, slot_tok_ref, slot_dst_ref,
                x_hbm, wg_ref, wu_ref, wd_ref, y_hbm,
                xbuf, obuf, wg_bf, wu_bf, wd_bf, gsem, ssem):
    b = pl.program_id(0)
    total = total_ref[0]
    slot = b % 2

    def gather_copy(blk, j, buf_slot):
        tok = slot_tok_ref[blk * MOE_BLOCK + j]
        return pltpu.make_async_copy(x_hbm.at[pl.ds(tok, 1)], xbuf.at[buf_slot, pl.ds(j, 1)],
                                     gsem.at[buf_slot])

    def scatter_copy(blk, j, buf_slot):
        dst = slot_dst_ref[blk * MOE_BLOCK + j]
        return pltpu.make_async_copy(obuf.at[buf_slot, pl.ds(j, 1)], y_hbm.at[pl.ds(dst, 1)],
                                     ssem.at[buf_slot])

    def wait_gather(buf_slot):
        pltpu.make_async_copy(x_hbm.at[pl.ds(0, MOE_BLOCK)], xbuf.at[buf_slot],
                              gsem.at[buf_slot]).wait()

    def wait_scatter(buf_slot):
        pltpu.make_async_copy(obuf.at[buf_slot], y_hbm.at[pl.ds(0, MOE_BLOCK)],
                              ssem.at[buf_slot]).wait()

    @pl.when(b == 0)
    def _():
        n_real = y_hbm.shape[0] - 2 * MOE_BLOCK
        obuf[0] = jnp.zeros(obuf.shape[1:], U32)
        for i in range(2):
            dump = y_hbm.at[pl.ds(n_real + i * MOE_BLOCK, MOE_BLOCK)]
            pltpu.make_async_copy(obuf.at[0], dump, ssem.at[i]).start()
        for i in range(2):
            wait_scatter(i)
        for j in range(MOE_BLOCK):
            gather_copy(0, j, 0).start()

    @pl.when(b < total)
    def _():
        @pl.when(blk_first_ref[b] == 1)
        def _():
            wg_bf[...] = wg_ref[...].astype(BF16)
            wu_bf[...] = wu_ref[...].astype(BF16)
            wd_bf[...] = wd_ref[...].astype(BF16)

        wait_gather(slot)
        for j in range(MOE_BLOCK):
            gather_copy(b + 1, j, 1 - slot).start(priority=j % 2)
        h_lo, h_hi = _unpack_halves(xbuf[slot])
        hb = jnp.concatenate([h_lo.astype(BF16), h_hi.astype(BF16)], axis=1)
        gt = jnp.dot(hb, wg_bf[...], preferred_element_type=F32)
        up = jnp.dot(hb, wu_bf[...], preferred_element_type=F32)
        hid = gt * (1.0 / (1.0 + jnp.exp(-gt))) * up
        obuf[slot] = _pack_halves(jnp.dot(hid.astype(BF16), wd_bf[...],
                                          preferred_element_type=F32))
        for j in range(MOE_BLOCK):
            scatter_copy(b, j, slot).start(priority=j % 2)

        @pl.when(b >= 1)
        def _():
            wait_scatter(1 - slot)

        @pl.when(b == total - 1)
        def _():
            wait_scatter(slot)
            wait_gather(1 - slot)


def _moe_dispatch(ids, n_experts):
    s, k = ids.shape
    a = s * k
    n_blocks = -(-a // MOE_BLOCK) + n_experts
    e_flat = ids.reshape(a)
    order = jnp.argsort(e_flat, stable=True).astype(jnp.int32)
    experts = jnp.arange(n_experts, dtype=jnp.int32)
    counts = jnp.sum((e_flat[:, None] == experts[None, :]).astype(jnp.int32), axis=0)
    starts = jnp.cumsum(counts) - counts
    nblk = (counts + MOE_BLOCK - 1) // MOE_BLOCK
    blk_end = jnp.cumsum(nblk)
    blk_start = blk_end - nblk
    total = blk_end[-1]
    bidx = jnp.arange(n_blocks, dtype=jnp.int32)
    used = bidx < total
    b_eff = jnp.minimum(bidx, total - 1)
    blk_e = jnp.sum((blk_end[None, :] <= b_eff[:, None]).astype(jnp.int32), axis=1)
    blk_e = jnp.minimum(blk_e, n_experts - 1)
    rank0 = (bidx - blk_start[blk_e]) * MOE_BLOCK
    blk_first = (used & (rank0 == 0)).astype(jnp.int32)
    lane = jnp.arange(MOE_BLOCK, dtype=jnp.int32)[None, :]
    rank = rank0[:, None] + lane
    valid = used[:, None] & (rank < counts[blk_e][:, None])
    src = jnp.clip(starts[blk_e][:, None] + rank, 0, a - 1)
    asg = order[src]
    slot_tok = jnp.where(valid, asg // k, 0)
    slot_tok = jnp.concatenate([slot_tok, jnp.zeros((1, MOE_BLOCK), jnp.int32)]).reshape(-1)
    dump = a + (bidx[:, None] % 2) * MOE_BLOCK + lane
    slot_dst = jnp.where(valid, (asg % k) * s + asg // k, dump).reshape(-1)
    return blk_e, total.reshape(1), blk_first, slot_tok, slot_dst, n_blocks


def _moe_experts(hfp, ids, layer, w_gate, w_up, w_down):
    s, dp = hfp.shape
    d = 2 * dp
    n_experts, _, f = w_gate.shape[1:]
    k = ids.shape[1]
    blk_e, total, blk_first, slot_tok, slot_dst, n_blocks = _moe_dispatch(ids, n_experts)
    w_idx = lambda b, be, *_: (layer, be[b], 0, 0)
    return pl.pallas_call(
        _moe_kernel,
        out_shape=jax.ShapeDtypeStruct((s * k + 2 * MOE_BLOCK, dp), U32),
        grid_spec=pltpu.PrefetchScalarGridSpec(
            num_scalar_prefetch=5,
            grid=(n_blocks,),
            in_specs=[pl.BlockSpec(memory_space=pl.ANY),
                      pl.BlockSpec((None, None, d, f), w_idx),
                      pl.BlockSpec((None, None, d, f), w_idx),
                      pl.BlockSpec((None, None, f, d), w_idx)],
            out_specs=pl.BlockSpec(memory_space=pl.ANY),
            scratch_shapes=[pltpu.VMEM((2, MOE_BLOCK, dp), U32),
                            pltpu.VMEM((2, MOE_BLOCK, dp), U32),
                            pltpu.VMEM((d, f), BF16),
                            pltpu.VMEM((d, f), BF16),
                            pltpu.VMEM((f, d), BF16),
                            pltpu.SemaphoreType.DMA((2,)),
                            pltpu.SemaphoreType.DMA((2,))]),
        compiler_params=_params(("arbitrary",)),
        name="moe_experts",
    )(blk_e, total, blk_first, slot_tok, slot_dst, hfp, w_gate, w_up, w_down)


def _vec_pack(d, **rows):
    out = jnp.zeros((8, d), F32)
    for name, v in rows.items():
        out = out.at[name_to_row[name]].set(v)
    return out


name_to_row = {"gate": _V_GATE, "gain": _V_GAIN, "bias": _V_BIAS, "scale_a": _V_SCALE_A,
               "shift_a": _V_SHIFT_A, "scale_b": _V_SCALE_B, "shift_b": _V_SHIFT_B}


def kernel(x, c, ada_down, ada_up, ada_bias, ln_gain, ln_bias, a_w_qkv, a_w_o, kv_ada_down, kv_ada_up, kv_ada_bias, kv_w, b_w_q, b_w_o, moe_w_group, moe_b_group, moe_w_expert, moe_b_expert, moe_w_gate, moe_w_up, moe_w_down):
    batch, s, d = x.shape
    assert batch == 1
    depth = ada_down.shape[0]
    n_a = a_w_qkv.shape[0]
    alpha = (2 * depth) ** 0.25
    a_heads = a_w_o.shape[1] // HEAD_DIM
    a_width = a_heads * HEAD_DIM
    n_groups = moe_w_group.shape[-1]
    n_per_group = moe_w_expert.shape[-1]
    kv_heads = kv_w.shape[1] // (2 * HEAD_DIM)
    kv_width = kv_heads * HEAD_DIM
    b_heads = kv_heads * MOBA_REP
    n_blk = s // MOBA_BLOCK

    mods = _modulation(c, ada_down, ada_up, ada_bias).reshape(depth, 6, d)
    kv_mod = _modulation(c, kv_ada_down[None], kv_ada_up[None], kv_ada_bias[None]).reshape(2, d)

    x2 = x.reshape(s, d)
    hs = _prologue(x2, _vec_pack(d, scale_a=mods[0, 1], shift_a=mods[0, 0]))

    k_blocks = vt_blocks = k_mean = slope_lanes = None
    for l in range(depth):
        shift1, scale1, gate1, shift2, scale2, gate2 = [mods[l, i] for i in range(6)]
        if l < n_a:
            outs, lses = [], []
            for g, (_, dil) in enumerate(A_GROUPS):
                qkv = _matmul(hs, a_w_qkv, l, BF16, n_off=g * 3 * a_width, n_out=3 * a_width,
                              dil=dil)
                o_g, lse_g = _dilated_attention(qkv, dil, a_heads)
                outs.append(o_g)
                lses.append(lse_g)
            merged = _merge_groups(outs, lses, a_heads)
            mix = _matmul(merged, a_w_o, l, BF16)
        else:
            j = l - n_a
            q = _matmul(hs, b_w_q, j, F32)
            o = _moba_attention(q, k_blocks, vt_blocks, k_mean, slope_lanes)
            mix = _matmul(o, b_w_o, j, BF16)

        wr = jnp.concatenate(
            [moe_w_group[l], jnp.moveaxis(moe_w_expert[l], 0, 1).reshape(d, n_groups * n_per_group),
             jnp.zeros((d, LANES - n_groups * (1 + n_per_group)), F32)], axis=1)
        br = jnp.concatenate(
            [moe_b_group[l], moe_b_expert[l].reshape(-1),
             jnp.zeros((LANES - n_groups * (1 + n_per_group),), F32)]).reshape(1, LANES)
        vec1 = _vec_pack(d, gate=gate1, gain=ln_gain[l, 0], bias=ln_bias[l, 0],
                         scale_a=scale2, shift_a=shift2)
        x2, route, hfp = _ln_router(x2, mix, vec1, wr, br, alpha, n_groups, n_per_group)
        ids = route[:, :MOE_TOPK].astype(jnp.int32)
        y2 = _moe_experts(hfp, ids, l, moe_w_gate, moe_w_up, moe_w_down)

        last = l == depth - 1
        with_kv = l == n_a - 1 and n_a < depth
        rows = dict(gate=gate2, gain=ln_gain[l, 1], bias=ln_bias[l, 1])
        if not last:
            rows.update(scale_a=mods[l + 1, 1], shift_a=mods[l + 1, 0])
        if with_kv:
            rows.update(scale_b=kv_mod[1], shift_b=kv_mod[0])
        res = _ln_combine(x2, y2, route, _vec_pack(d, **rows), alpha, not last, with_kv)
        x2 = res[0]
        if with_kv:
            kv = _matmul(res[1], kv_w[None], 0, F32)
            k_f32 = kv[:, :kv_width]
            k_mean = _block_means(k_f32).reshape(n_blk, kv_heads, HEAD_DIM).transpose(1, 0, 2)
            k_blocks = k_f32.astype(BF16).reshape(n_blk, MOBA_BLOCK, kv_heads, HEAD_DIM) \
                .transpose(2, 0, 1, 3)
            vt_blocks = kv[:, kv_width:].astype(BF16).reshape(n_blk, MOBA_BLOCK, kv_heads, HEAD_DIM) \
                .transpose(2, 0, 3, 1)
            slopes = jnp.exp2(-8.0 * jnp.arange(1, b_heads + 1, dtype=F32) / b_heads)
            slope_lanes = jnp.repeat(slopes.reshape(kv_heads, MOBA_REP), MOBA_BLOCK, axis=1) \
                .reshape(kv_heads, 1, MOBA_REP * MOBA_BLOCK)
        if not last:
            hs = res[-1]
    return x2.reshape(batch, s, d)
```

```python
import functools
import math

import numpy as np
import jax
import jax.numpy as jnp
from jax import lax
from jax.experimental import pallas as pl
from jax.experimental.pallas import tpu as pltpu

F32 = jnp.float32
BF16 = jnp.bfloat16
U32 = jnp.uint32
HIGHEST = lax.Precision.HIGHEST

HEAD_DIM = 128
A_GROUPS = ((128, 1), (512, 4), (2048, 16))
A_BLOCK = 128
MOBA_BLOCK = 256
MOBA_TOPK = 3
MOBA_REP = 4
MOBA_UNROLL = 8
MOE_TOPK = 2
MOE_BLOCK = 128
LN_EPS = 1e-5
NEG = -1e30
LOG2E = math.log2(math.e)
LANES = 128
VMEM_LIMIT = 56 * 1024 * 1024

_NT = (((1,), (1,)), ((), ()))


def _params(sem, vmem=VMEM_LIMIT):
    return pltpu.CompilerParams(dimension_semantics=sem, vmem_limit_bytes=vmem)


def _mod_kernel(c_ref, wd_ref, wu_ref, b_ref, o_ref, t_ref):
    @pl.when(pl.program_id(1) == 0)
    def _():
        c = c_ref[...]
        sc = c * (1.0 / (1.0 + jnp.exp(-c)))
        t_ref[...] = jnp.dot(sc, wd_ref[...], preferred_element_type=F32, precision=HIGHEST)

    o_ref[...] = jnp.dot(t_ref[...], wu_ref[...], preferred_element_type=F32,
                         precision=HIGHEST) + b_ref[...]


def _modulation(c, w_down, w_up, b_up):
    n_l, d, r = w_down.shape
    n = w_up.shape[-1]
    tn = n // pl.cdiv(n, 6144)
    c8 = jnp.broadcast_to(c, (8, d))
    out = pl.pallas_call(
        _mod_kernel,
        out_shape=jax.ShapeDtypeStruct((n_l, 8, n), F32),
        grid=(n_l, n // tn),
        in_specs=[
            pl.BlockSpec((8, d), lambda l, j: (0, 0)),
            pl.BlockSpec((None, d, r), lambda l, j: (l, 0, 0)),
            pl.BlockSpec((None, r, tn), lambda l, j: (l, 0, j)),
            pl.BlockSpec((None, 1, tn), lambda l, j: (l, 0, j)),
        ],
        out_specs=pl.BlockSpec((None, 8, tn), lambda l, j: (l, 0, j)),
        scratch_shapes=[pltpu.VMEM((8, r), F32)],
        compiler_params=_params(("arbitrary", "arbitrary")),
        name="modulation",
    )(c8, w_down, w_up, b_up.reshape(n_l, 1, n))
    return out[:, 0, :]


def _mm_kernel(a_ref, b_ref, o_ref, b_bf_ref):
    @pl.when(pl.program_id(1) == 0)
    def _():
        b_bf_ref[...] = b_ref[...].astype(BF16)

    o_ref[...] = jnp.dot(a_ref[...], b_bf_ref[...],
                         preferred_element_type=F32).astype(o_ref.dtype)


def _matmul_blocks(out_dtype):
    return (512, 1024) if jnp.dtype(out_dtype).itemsize == 2 else (1024, 512)


def _matmul(a, b, layer, out_dtype, n_off=0, n_out=None, dil=1):
    m, k = a.shape
    tm, tn = _matmul_blocks(out_dtype)
    n_out = b.shape[2] if n_out is None else n_out
    tm = min(tm, m // dil)
    tn = min(tn, n_out)
    off = n_off // tn
    bps = m // dil // tm
    return pl.pallas_call(
        _mm_kernel,
        out_shape=jax.ShapeDtypeStruct((m, n_out), out_dtype),
        grid=(n_out // tn, m // tm),
        in_specs=[
            pl.BlockSpec((tm, k), lambda j, i: (i % bps, i // bps)),
            pl.BlockSpec((None, k, tn), lambda j, i: (layer, 0, j + off)),
        ],
        out_specs=pl.BlockSpec((tm, tn), lambda j, i: (i, j)),
        scratch_shapes=[pltpu.VMEM((k, tn), BF16)],
        compiler_params=_params(("parallel", "arbitrary")),
        name="matmul",
    )(a.reshape(m // dil, dil * k), b)


def _pack_halves(v):
    n = v.shape[1] // 2
    hi = lax.bitcast_convert_type(v[:, :n].astype(BF16).astype(F32), U32)
    lo = lax.bitcast_convert_type(v[:, n:].astype(BF16).astype(F32), U32)
    return (hi & jnp.uint32(0xFFFF0000)) | (lo >> 16)


def _unpack_halves(p):
    return (lax.bitcast_convert_type(p & jnp.uint32(0xFFFF0000), F32),
            lax.bitcast_convert_type(p << 16, F32))


def _layer_norm(v, gain, bias):
    mu = jnp.mean(v, axis=-1, keepdims=True)
    vc = v - mu
    var = jnp.mean(vc * vc, axis=-1, keepdims=True)
    return vc * lax.rsqrt(var + LN_EPS) * gain + bias


_V_GATE, _V_GAIN, _V_BIAS, _V_SCALE_A, _V_SHIFT_A, _V_SCALE_B, _V_SHIFT_B = range(7)


def _prologue_kernel(x_ref, vec_ref, o_ref):
    h = x_ref[...] * (1.0 + vec_ref[_V_SCALE_A:_V_SCALE_A + 1, :]) \
        + vec_ref[_V_SHIFT_A:_V_SHIFT_A + 1, :]
    o_ref[...] = h.astype(BF16)


def _prologue(x2, vecs, tm=256):
    s, d = x2.shape
    return pl.pallas_call(
        _prologue_kernel,
        out_shape=jax.ShapeDtypeStruct((s, d), BF16),
        grid=(s // tm,),
        in_specs=[pl.BlockSpec((tm, d), lambda i: (i, 0)),
                  pl.BlockSpec((8, d), lambda i: (0, 0))],
        out_specs=pl.BlockSpec((tm, d), lambda i: (i, 0)),
        compiler_params=_params(("parallel",)),
        name="prologue",
    )(x2, vecs)


def _ln_router_kernel(alpha, n_groups, n_per_group,
                      x_ref, mix_ref, vec_ref, wrh_ref, wrl_ref, br_ref, xo_ref, route_ref, hfp_ref):
    gate = vec_ref[_V_GATE:_V_GATE + 1, :]
    x1 = alpha * x_ref[...] + (1.0 + gate) * mix_ref[...].astype(F32)
    xn = _layer_norm(x1, vec_ref[_V_GAIN:_V_GAIN + 1, :], vec_ref[_V_BIAS:_V_BIAS + 1, :])
    xo_ref[...] = xn
    hf = xn * (1.0 + vec_ref[_V_SCALE_A:_V_SCALE_A + 1, :]) + vec_ref[_V_SHIFT_A:_V_SHIFT_A + 1, :]
    hfp_ref[...] = _pack_halves(hf)
    hf_hi = hf.astype(BF16)
    hf_lo = (hf - hf_hi.astype(F32)).astype(BF16)
    logits = jnp.dot(hf_hi, wrh_ref[...], preferred_element_type=F32) \
        + jnp.dot(hf_hi, wrl_ref[...], preferred_element_type=F32) \
        + jnp.dot(hf_lo, wrh_ref[...], preferred_element_type=F32) + br_ref[...]
    lane = lax.broadcasted_iota(jnp.int32, logits.shape, 1).astype(F32)
    ninf = -jnp.inf
    big = float(LANES)
    is_g = lane < n_groups
    gl = jnp.where(is_g, logits, ninf)
    gmax = jnp.max(gl, axis=-1, keepdims=True)
    gsel = jnp.min(jnp.where(gl == gmax, lane, big), axis=-1, keepdims=True)
    p_group = 1.0 / jnp.sum(jnp.where(is_g, jnp.exp(logits - gmax), 0.0), axis=-1, keepdims=True)
    lo = n_groups + gsel * n_per_group
    el = jnp.where((lane >= lo) & (lane < lo + n_per_group), logits, ninf)
    m1 = jnp.max(el, axis=-1, keepdims=True)
    i1 = jnp.min(jnp.where(el == m1, lane, big), axis=-1, keepdims=True)
    el2 = jnp.where(lane == i1, ninf, el)
    m2 = jnp.max(el2, axis=-1, keepdims=True)
    i2 = jnp.min(jnp.where(el2 == m2, lane, big), axis=-1, keepdims=True)
    e21 = jnp.exp(m2 - m1)
    inv = 1.0 / (1.0 + e21)
    w1 = p_group * inv
    w2 = p_group * (e21 * inv)
    route_ref[...] = jnp.where(lane == 0.0, i1 - n_groups,
                     jnp.where(lane == 1.0, i2 - n_groups,
                     jnp.where(lane == 2.0, w1,
                     jnp.where(lane == 3.0, w2, 0.0))))


def _ln_router(x2, mix, vecs, wr, br, alpha, n_groups, n_per_group, tm=256):
    s, d = x2.shape
    wr_top = lax.bitcast_convert_type(
        lax.bitcast_convert_type(wr, U32) & jnp.uint32(0xFFFF0000), F32)
    wr_hi = wr_top.astype(BF16)
    wr_lo = (wr - wr_top).astype(BF16)
    return pl.pallas_call(
        functools.partial(_ln_router_kernel, alpha, n_groups, n_per_group),
        out_shape=[jax.ShapeDtypeStruct((s, d), F32),
                   jax.ShapeDtypeStruct((s, LANES), F32),
                   jax.ShapeDtypeStruct((s, d // 2), U32)],
        grid=(s // tm,),
        in_specs=[pl.BlockSpec((tm, d), lambda i: (i, 0)),
                  pl.BlockSpec((tm, d), lambda i: (i, 0)),
                  pl.BlockSpec((8, d), lambda i: (0, 0)),
                  pl.BlockSpec((d, LANES), lambda i: (0, 0)),
                  pl.BlockSpec((d, LANES), lambda i: (0, 0)),
                  pl.BlockSpec((1, LANES), lambda i: (0, 0))],
        out_specs=[pl.BlockSpec((tm, d), lambda i: (i, 0)),
                   pl.BlockSpec((tm, LANES), lambda i: (i, 0)),
                   pl.BlockSpec((tm, d // 2), lambda i: (i, 0))],
        compiler_params=_params(("parallel",)),
        name="ln_router",
    )(x2, mix, vecs, wr_hi, wr_lo, br)


def _ln_combine_kernel(alpha, with_next, with_kv, x_ref, y0_ref, y1_ref, route_ref, vec_ref, *refs):
    w1, w2 = route_ref[:, 2:3], route_ref[:, 3:4]
    a0, b0 = _unpack_halves(y0_ref[...])
    a1, b1 = _unpack_halves(y1_ref[...])
    ffn = jnp.concatenate([a0 * w1 + a1 * w2, b0 * w1 + b1 * w2], axis=1)
    gate = vec_ref[_V_GATE:_V_GATE + 1, :]
    x1 = alpha * x_ref[...] + (1.0 + gate) * ffn
    xn = _layer_norm(x1, vec_ref[_V_GAIN:_V_GAIN + 1, :], vec_ref[_V_BIAS:_V_BIAS + 1, :])
    refs[0][...] = xn
    k = 1
    if with_kv:
        hk = xn * (1.0 + vec_ref[_V_SCALE_B:_V_SCALE_B + 1, :]) + vec_ref[_V_SHIFT_B:_V_SHIFT_B + 1, :]
        refs[k][...] = hk.astype(BF16)
        k += 1
    if with_next:
        h = xn * (1.0 + vec_ref[_V_SCALE_A:_V_SCALE_A + 1, :]) + vec_ref[_V_SHIFT_A:_V_SHIFT_A + 1, :]
        refs[k][...] = h.astype(BF16)


def _ln_combine(x2, y2, route, vecs, alpha, with_next, with_kv, tm=256):
    s, d = x2.shape
    row = lambda i: (i, 0)
    n_bf = int(with_kv) + int(with_next)
    shapes = [jax.ShapeDtypeStruct((s, d), F32)] + [jax.ShapeDtypeStruct((s, d), BF16)] * n_bf
    specs = [pl.BlockSpec((tm, d), row)] * (1 + n_bf)
    return pl.pallas_call(
        functools.partial(_ln_combine_kernel, alpha, with_next, with_kv),
        out_shape=shapes,
        grid=(s // tm,),
        in_specs=[pl.BlockSpec((tm, d), row),
                  pl.BlockSpec((tm, d // 2), row),
                  pl.BlockSpec((tm, d // 2), lambda i: (i + s // tm, 0)),
                  pl.BlockSpec((tm, LANES), row),
                  pl.BlockSpec((8, d), lambda i: (0, 0))],
        out_specs=specs,
        compiler_params=_params(("parallel",)),
        name="ln_combine",
    )(x2, y2, y2, route, vecs)


def _dilated_kernel(dil, blocks_per_stream, slopes, q_ref, kp_ref, kc_ref, vp_ref, vc_ref,
                    o_ref, lse_ref):
    nb = pl.program_id(0) % blocks_per_stream
    has_prev = nb > 0
    qi = lax.broadcasted_iota(jnp.int32, (A_BLOCK, A_BLOCK), 0)
    kj = lax.broadcasted_iota(jnp.int32, (A_BLOCK, A_BLOCK), 1)
    dist_prev = jnp.where((kj >= qi) & has_prev, ((qi + A_BLOCK - kj) * dil).astype(F32), -NEG)
    dist_cur = jnp.where(kj <= qi, ((qi - kj) * dil).astype(F32), -NEG)
    sqrt_e = math.sqrt(HEAD_DIM)
    lane = lax.broadcasted_iota(jnp.int32, (A_BLOCK, LANES), 1)
    ones = jnp.ones((A_BLOCK, LANES), BF16)
    n_heads = len(slopes)

    def scores(h):
        hs = slice(h * HEAD_DIM, (h + 1) * HEAD_DIM)
        q = q_ref[:, hs]
        return (lax.dot_general(q, kp_ref[:, hs], _NT, preferred_element_type=F32),
                lax.dot_general(q, kc_ref[:, hs], _NT, preferred_element_type=F32))

    ahead = 6
    pending = {h: scores(h) for h in range(min(ahead, n_heads))}
    lse_all = jnp.zeros((A_BLOCK, LANES), F32)
    for h, slope in enumerate(slopes):
        hs = slice(h * HEAD_DIM, (h + 1) * HEAD_DIM)
        s_p, s_c = pending.pop(h)
        t_p = s_p - (slope * sqrt_e) * dist_prev
        t_c = s_c - (slope * sqrt_e) * dist_cur
        t_max = jnp.max(jnp.maximum(t_p, t_c), axis=-1, keepdims=True)
        p_p = jnp.exp2((t_p - t_max) * (LOG2E / sqrt_e)).astype(BF16)
        p_c = jnp.exp2((t_c - t_max) * (LOG2E / sqrt_e)).astype(BF16)
        if h + ahead < n_heads:
            pending[h + ahead] = scores(h + ahead)
        o = jnp.dot(p_p, vp_ref[:, hs], preferred_element_type=F32) \
            + jnp.dot(p_c, vc_ref[:, hs], preferred_element_type=F32)
        l = jnp.dot(p_p, ones, preferred_element_type=F32) \
            + jnp.dot(p_c, ones, preferred_element_type=F32)
        o_ref[:, hs] = o * (1.0 / l)
        lse_all = jnp.where(lane == h, t_max * (1.0 / sqrt_e) + jnp.log(l), lse_all)
    lse_ref[...] = lse_all


def _dilated_attention(qkv, dil, n_heads):
    s = qkv.shape[0]
    w = n_heads * HEAD_DIM
    n = s // dil
    bps = n // A_BLOCK
    slopes = [float(v) for v in np.exp2(-8.0 * np.arange(1, n_heads + 1, dtype=np.float32) / n_heads)]
    prev = lambda b: (jnp.maximum(b - 1, 0), 1)
    o, lse = pl.pallas_call(
        functools.partial(_dilated_kernel, dil, bps, slopes),
        out_shape=[jax.ShapeDtypeStruct((n, dil * w), F32),
                   jax.ShapeDtypeStruct((n, dil * LANES), F32)],
        grid=(s // A_BLOCK,),
        in_specs=[pl.BlockSpec((A_BLOCK, w), lambda b: (b, 0)),
                  pl.BlockSpec((A_BLOCK, w), prev),
                  pl.BlockSpec((A_BLOCK, w), lambda b: (b, 1)),
                  pl.BlockSpec((A_BLOCK, w), lambda b: (jnp.maximum(b - 1, 0), 2)),
                  pl.BlockSpec((A_BLOCK, w), lambda b: (b, 2))],
        out_specs=[pl.BlockSpec((A_BLOCK, w), lambda b: (b % bps, b // bps)),
                   pl.BlockSpec((A_BLOCK, LANES), lambda b: (b % bps, b // bps))],
        compiler_params=_params(("parallel",)),
        name="dilated_attention",
    )(qkv, qkv, qkv, qkv, qkv)
    return o, lse


def _merge_kernel(n_heads, *refs):
    n_g = len(A_GROUPS)
    o_refs, l_refs = refs[:n_g], refs[n_g:2 * n_g]
    out_ref = refs[2 * n_g]
    stages = iter(refs[2 * n_g + 1:])
    tm = out_ref.shape[0]
    o_nat, l_nat = [], []
    for (_, dil), o_ref, l_ref in zip(A_GROUPS, o_refs, l_refs):
        if dil == 1:
            o_nat.append(lambda h, o_ref=o_ref: o_ref[:, h * HEAD_DIM:(h + 1) * HEAD_DIM])
            l_nat.append(l_ref[...])
            continue
        stage = next(stages)
        rows = tm // dil
        w = o_ref.shape[1] // dil
        for r in range(dil):
            for h in range(n_heads):
                stage[h, pl.ds(r, rows, stride=dil), :] = \
                    o_ref[:, r * w + h * HEAD_DIM:r * w + (h + 1) * HEAD_DIM]
            stage[n_heads, pl.ds(r, rows, stride=dil), :] = l_ref[:, r * LANES:(r + 1) * LANES]
        o_nat.append(lambda h, stage=stage: stage[h])
        l_nat.append(stage[n_heads])
    m = functools.reduce(jnp.maximum, l_nat)
    es = [jnp.exp(l - m) for l in l_nat]
    inv = 1.0 / functools.reduce(lambda a, b: a + b, es)
    ws = [e * inv for e in es]
    for h in range(n_heads):
        acc = functools.reduce(lambda a, b: a + b,
                               [o(h) * w_g[:, h:h + 1] for o, w_g in zip(o_nat, ws)])
        out_ref[:, h * HEAD_DIM:(h + 1) * HEAD_DIM] = acc.astype(BF16)


def _merge_groups(os_, lses, n_heads, tm=256):
    w = n_heads * HEAD_DIM
    s = os_[0].shape[0] * A_GROUPS[0][1]
    row = lambda i: (i, 0)
    dils = [dil for _, dil in A_GROUPS]
    return pl.pallas_call(
        functools.partial(_merge_kernel, n_heads),
        out_shape=jax.ShapeDtypeStruct((s, w), BF16),
        grid=(s // tm,),
        in_specs=[pl.BlockSpec((tm // dil, dil * w), row) for dil in dils]
        + [pl.BlockSpec((tm // dil, dil * LANES), row) for dil in dils],
        out_specs=pl.BlockSpec((tm, w), row),
        scratch_shapes=[pltpu.VMEM((n_heads + 1, tm, LANES), F32) for dil in dils if dil > 1],
        compiler_params=_params(("parallel",)),
        name="merge_groups",
    )(*os_, *lses)


def _kmean_kernel(k_ref, o_ref):
    o_ref[...] = jnp.mean(k_ref[...], axis=0, keepdims=True)


def _block_means(k):
    s, kw = k.shape
    nb = s // MOBA_BLOCK
    return pl.pallas_call(
        _kmean_kernel,
        out_shape=jax.ShapeDtypeStruct((nb, 1, kw), F32),
        grid=(nb,),
        in_specs=[pl.BlockSpec((MOBA_BLOCK, kw), lambda i: (i, 0))],
        out_specs=pl.BlockSpec((None, 1, kw), lambda i: (i, 0, 0)),
        compiler_params=_params(("parallel",)),
        name="block_means",
    )(k)


def _moba_kernel(q_ref, k_ref, vt_ref, kmean_ref, slope_ref, o_ref,
                 qs_ref, sel_ref, bias_ref, m_ref, l_ref, acc_ref):
    qb = pl.program_id(1)
    n_blk = kmean_ref.shape[0]
    rows = MOBA_REP * MOBA_BLOCK
    n_chunks = rows // LANES
    slope2 = slope_ref[...] * LOG2E

    @pl.when(qb == 0)
    def _():
        lane_t = (lax.broadcasted_iota(jnp.int32, (1, rows), 1) % MOBA_BLOCK).astype(F32)
        sub = lax.broadcasted_iota(jnp.int32, (MOBA_BLOCK, 1), 0).astype(F32)
        bias_ref[...] = slope2 * (lane_t - sub)

    q = q_ref[...]
    q4 = jnp.concatenate([q[:, r * HEAD_DIM:(r + 1) * HEAD_DIM] for r in range(MOBA_REP)], axis=0)
    gate = lax.dot_general(kmean_ref[...], q4, _NT, preferred_element_type=F32, precision=HIGHEST)
    blk = lax.broadcasted_iota(jnp.int32, (n_blk, rows), 0).astype(F32)
    ninf = -jnp.inf
    gate = jnp.where(blk < qb.astype(F32), gate, ninf)
    sel = jnp.zeros((n_blk, rows), F32)
    for _ in range(MOBA_TOPK):
        top = jnp.max(gate, axis=0, keepdims=True)
        idx = jnp.min(jnp.where(gate == top, blk, float(n_blk)), axis=0, keepdims=True)
        hit = (blk == idx) & (top > ninf)
        sel = jnp.where(hit, 1.0, sel)
        gate = jnp.where(hit, ninf, gate)
    sel_ref[...] = sel
    qs_ref[...] = (q4 * (LOG2E / math.sqrt(HEAD_DIM))).astype(BF16)

    kk = lax.broadcasted_iota(jnp.int32, (MOBA_BLOCK, LANES), 0)
    tt = lax.broadcasted_iota(jnp.int32, (MOBA_BLOCK, LANES), 1)

    def run_blocks(blocks, own):
        k_b = [k_ref[n] for n in blocks]
        vt_b = [vt_ref[n] for n in blocks]
        sel_b = [None if own else sel_ref[pl.ds(n, 1), :] for n in blocks]
        bt_b = [None if own else slope2 * ((n - qb) * MOBA_BLOCK).astype(F32) for n in blocks]
        tasks = [(bi, c) for bi in range(len(blocks)) for c in range(n_chunks)]

        def score(t):
            bi, c = tasks[t]
            return lax.dot_general(k_b[bi], qs_ref[c * LANES:(c + 1) * LANES, :], _NT,
                                   preferred_element_type=F32)

        ahead = 2
        pending = {t: score(t) for t in range(min(ahead, len(tasks)))}
        for t, (bi, c) in enumerate(tasks):
            cs = slice(c * LANES, (c + 1) * LANES)
            u = pending.pop(t) - bias_ref[:, cs]
            if own:
                u = jnp.where(kk <= tt + (c * LANES) % MOBA_BLOCK, u, NEG)
                m_new = jnp.max(u, axis=0, keepdims=True)
                p = jnp.exp2(u - m_new)
                l_new = jnp.sum(p, axis=0, keepdims=True)
            else:
                picked = sel_b[bi][:, cs] > 0.0
                bt = bt_b[bi][:, cs]
                m_old = m_ref[:, cs]
                m_new = jnp.maximum(
                    m_old, jnp.where(picked, jnp.max(u, axis=0, keepdims=True) + bt, NEG))
                p = jnp.exp2(u - jnp.where(picked, m_new - bt, -NEG))
                alpha = jnp.exp2(m_old - m_new)
                l_new = alpha * l_ref[:, cs] + jnp.sum(p, axis=0, keepdims=True)
            m_ref[:, cs] = m_new
            l_ref[:, cs] = l_new
            if t + ahead < len(tasks):
                pending[t + ahead] = score(t + ahead)
            pv = jnp.dot(vt_b[bi], p.astype(BF16), preferred_element_type=F32)
            acc_ref[:, cs] = pv if own else alpha * acc_ref[:, cs] + pv

    run_blocks([qb], own=True)

    def body(i, carry):
        run_blocks([i * MOBA_UNROLL + j for j in range(MOBA_UNROLL)], own=False)
        return carry

    n_full = qb // MOBA_UNROLL
    lax.fori_loop(0, n_full, body, 0)
    base = n_full * MOBA_UNROLL
    rem = qb - base
    size = MOBA_UNROLL // 2
    while size >= 1:
        start = base + (rem // (2 * size)) * (2 * size)

        @pl.when((rem // size) % 2 == 1)
        def _(start=start, size=size):
            run_blocks([start + j for j in range(size)], own=False)
        size //= 2

    o_t = acc_ref[...] * (1.0 / l_ref[...])
    for r in range(MOBA_REP):
        o_ref[:, r * HEAD_DIM:(r + 1) * HEAD_DIM] = \
            o_t[:, r * MOBA_BLOCK:(r + 1) * MOBA_BLOCK].T.astype(BF16)


def _moba_attention(q, k_blocks, vt_blocks, k_mean, slope_lanes):
    s = q.shape[0]
    g, nb = k_blocks.shape[:2]
    rows = MOBA_REP * MOBA_BLOCK
    qw = MOBA_REP * HEAD_DIM
    return pl.pallas_call(
        _moba_kernel,
        out_shape=jax.ShapeDtypeStruct((s, g * qw), BF16),
        grid=(g, nb),
        in_specs=[pl.BlockSpec((MOBA_BLOCK, qw), lambda gi, qb: (qb, gi)),
                  pl.BlockSpec((None, nb, MOBA_BLOCK, HEAD_DIM), lambda gi, qb: (gi, 0, 0, 0)),
                  pl.BlockSpec((None, nb, HEAD_DIM, MOBA_BLOCK), lambda gi, qb: (gi, 0, 0, 0)),
                  pl.BlockSpec((None, nb, HEAD_DIM), lambda gi, qb: (gi, 0, 0)),
                  pl.BlockSpec((None, 1, rows), lambda gi, qb: (gi, 0, 0))],
        out_specs=pl.BlockSpec((MOBA_BLOCK, qw), lambda gi, qb: (qb, gi)),
        scratch_shapes=[pltpu.VMEM((rows, HEAD_DIM), BF16),
                        pltpu.VMEM((nb, rows), F32),
                        pltpu.VMEM((MOBA_BLOCK, rows), F32),
                        pltpu.VMEM((1, rows), F32),
                        pltpu.VMEM((1, rows), F32),
                        pltpu.VMEM((HEAD_DIM, rows), F32)],
        compiler_params=_params(("parallel", "arbitrary")),
        name="moba_attention",
    )(q, k_blocks, vt_blocks, k_mean, slope_lanes)


def _moe_kernel(blk_e_ref, total_ref, blk_first_ref, slot_tok_ref, slot_dst_ref,
                x_hbm, wg_ref, wu_ref, wd_ref, y_hbm,
                xbuf, obuf, wg_bf, wu_bf, wd_bf, gsem, ssem):
    b = pl.program_id(0)
    total = total_ref[0]
    slot = b % 2

    def gather_copy(blk, j, buf_slot):
        tok = slot_tok_ref[blk * MOE_BLOCK + j]
        return pltpu.make_async_copy(x_hbm.at[pl.ds(tok, 1)], xbuf.at[buf_slot, pl.ds(j, 1)],
                                     gsem.at[buf_slot])

    def scatter_copy(blk, j, buf_slot):
        dst = slot_dst_ref[blk * MOE_BLOCK + j]
        return pltpu.make_async_copy(obuf.at[buf_slot, pl.ds(j, 1)], y_hbm.at[pl.ds(dst, 1)],
                                     ssem.at[buf_slot])

    def wait_gather(buf_slot):
        pltpu.make_async_copy(x_hbm.at[pl.ds(0, MOE_BLOCK)], xbuf.at[buf_slot],
                              gsem.at[buf_slot]).wait()

    def wait_scatter(buf_slot):
        pltpu.make_async_copy(obuf.at[buf_slot], y_hbm.at[pl.ds(0, MOE_BLOCK)],
                              ssem.at[buf_slot]).wait()

    @pl.when(b == 0)
    def _():
        n_real = y_hbm.shape[0] - 2 * MOE_BLOCK
        obuf[0] = jnp.zeros(obuf.shape[1:], U32)
        for i in range(2):
            dump = y_hbm.at[pl.ds(n_real + i * MOE_BLOCK, MOE_BLOCK)]
            pltpu.make_async_copy(obuf.at[0], dump, ssem.at[i]).start()
        for i in range(2):
            wait_scatter(i)
        for j in range(MOE_BLOCK):
            gather_copy(0, j, 0).start()

    @pl.when(b < total)
    def _():
        @pl.when(blk_first_ref[b] == 1)
        def _():
            wg_bf[...] = wg_ref[...].astype(BF16)
            wu_bf[...] = wu_ref[...].astype(BF16)
            wd_bf[...] = wd_ref[...].astype(BF16)

        wait_gather(slot)
        for j in range(MOE_BLOCK):
            gather_copy(b + 1, j, 1 - slot).start(priority=j % 2)
        h_lo, h_hi = _unpack_halves(xbuf[slot])
        hb = jnp.concatenate([h_lo.astype(BF16), h_hi.astype(BF16)], axis=1)
        gt = jnp.dot(hb, wg_bf[...], preferred_element_type=F32)
        up = jnp.dot(hb, wu_bf[...], preferred_element_type=F32)
        hid = gt * (1.0 / (1.0 + jnp.exp(-gt))) * up
        obuf[slot] = _pack_halves(jnp.dot(hid.astype(BF16), wd_bf[...],
                                          preferred_element_type=F32))
        for j in range(MOE_BLOCK):
            scatter_copy(b, j, slot).start(priority=j % 2)

        @pl.when(b >= 1)
        def _():
            wait_scatter(1 - slot)

        @pl.when(b == total - 1)
        def _():
            wait_scatter(slot)
            wait_gather(1 - slot)


def _moe_dispatch(ids, n_experts):
    s, k = ids.shape
    a = s * k
    n_blocks = -(-a // MOE_BLOCK) + n_experts
    e_flat = ids.reshape(a)
    order = jnp.argsort(e_flat, stable=True).astype(jnp.int32)
    experts = jnp.arange(n_experts, dtype=jnp.int32)
    counts = jnp.sum((e_flat[:, None] == experts[None, :]).astype(jnp.int32), axis=0)
    starts = jnp.cumsum(counts) - counts
    nblk = (counts + MOE_BLOCK - 1) // MOE_BLOCK
    blk_end = jnp.cumsum(nblk)
    blk_start = blk_end - nblk
    total = blk_end[-1]
    bidx = jnp.arange(n_blocks, dtype=jnp.int32)
    used = bidx < total
    b_eff = jnp.minimum(bidx, total - 1)
    blk_e = jnp.sum((blk_end[None, :] <= b_eff[:, None]).astype(jnp.int32), axis=1)
    blk_e = jnp.minimum(blk_e, n_experts - 1)
    rank0 = (bidx - blk_start[blk_e]) * MOE_BLOCK
    blk_first = (used & (rank0 == 0)).astype(jnp.int32)
    lane = jnp.arange(MOE_BLOCK, dtype=jnp.int32)[None, :]
    rank = rank0[:, None] + lane
    valid = used[:, None] & (rank < counts[blk_e][:, None])
    src = jnp.clip(starts[blk_e][:, None] + rank, 0, a - 1)
    asg = order[src]
    slot_tok = jnp.where(valid, asg // k, 0)
    slot_tok = jnp.concatenate([slot_tok, jnp.zeros((1, MOE_BLOCK), jnp.int32)]).reshape(-1)
    dump = a + (bidx[:, None] % 2) * MOE_BLOCK + lane
    slot_dst = jnp.where(valid, (asg % k) * s + asg // k, dump).reshape(-1)
    return blk_e, total.reshape(1), blk_first, slot_tok, slot_dst, n_blocks


def _moe_experts(hfp, ids, layer, w_gate, w_up, w_down):
    s, dp = hfp.shape
    d = 2 * dp
    n_experts, _, f = w_gate.shape[1:]
    k = ids.shape[1]
    blk_e, total, blk_first, slot_tok, slot_dst, n_blocks = _moe_dispatch(ids, n_experts)
    w_idx = lambda b, be, *_: (layer, be[b], 0, 0)
    return pl.pallas_call(
        _moe_kernel,
        out_shape=jax.ShapeDtypeStruct((s * k + 2 * MOE_BLOCK, dp), U32),
        grid_spec=pltpu.PrefetchScalarGridSpec(
            num_scalar_prefetch=5,
            grid=(n_blocks,),
            in_specs=[pl.BlockSpec(memory_space=pl.ANY),
                      pl.BlockSpec((None, None, d, f), w_idx),
                      pl.BlockSpec((None, None, d, f), w_idx),
                      pl.BlockSpec((None, None, f, d), w_idx)],
            out_specs=pl.BlockSpec(memory_space=pl.ANY),
            scratch_shapes=[pltpu.VMEM((2, MOE_BLOCK, dp), U32),
                            pltpu.VMEM((2, MOE_BLOCK, dp), U32),
                            pltpu.VMEM((d, f), BF16),
                            pltpu.VMEM((d, f), BF16),
                            pltpu.VMEM((f, d), BF16),
                            pltpu.SemaphoreType.DMA((2,)),
                            pltpu.SemaphoreType.DMA((2,))]),
        compiler_params=_params(("arbitrary",)),
        name="moe_experts",
    )(blk_e, total, blk_first, slot_tok, slot_dst, hfp, w_gate, w_up, w_down)


def _vec_pack(d, **rows):
    out = jnp.zeros((8, d), F32)
    for name, v in rows.items():
        out = out.at[name_to_row[name]].set(v)
    return out


name_to_row = {"gate": _V_GATE, "gain": _V_GAIN, "bias": _V_BIAS, "scale_a": _V_SCALE_A,
               "shift_a": _V_SHIFT_A, "scale_b": _V_SCALE_B, "shift_b": _V_SHIFT_B}


def kernel(x, c, ada_down, ada_up, ada_bias, ln_gain, ln_bias, a_w_qkv, a_w_o, kv_ada_down, kv_ada_up, kv_ada_bias, kv_w, b_w_q, b_w_o, moe_w_group, moe_b_group, moe_w_expert, moe_b_expert, moe_w_gate, moe_w_up, moe_w_down):
    batch, s, d = x.shape
    assert batch == 1
    depth = ada_down.shape[0]
    n_a = a_w_qkv.shape[0]
    alpha = (2 * depth) ** 0.25
    a_heads = a_w_o.shape[1] // HEAD_DIM
    a_width = a_heads * HEAD_DIM
    n_groups = moe_w_group.shape[-1]
    n_per_group = moe_w_expert.shape[-1]
    kv_heads = kv_w.shape[1] // (2 * HEAD_DIM)
    kv_width = kv_heads * HEAD_DIM
    b_heads = kv_heads * MOBA_REP
    n_blk = s // MOBA_BLOCK

    mods = _modulation(c, ada_down, ada_up, ada_bias).reshape(depth, 6, d)
    kv_mod = _modulation(c, kv_ada_down[None], kv_ada_up[None], kv_ada_bias[None]).reshape(2, d)

    x2 = x.reshape(s, d)
    hs = _prologue(x2, _vec_pack(d, scale_a=mods[0, 1], shift_a=mods[0, 0]))

    k_blocks = vt_blocks = k_mean = slope_lanes = None
    for l in range(depth):
        shift1, scale1, gate1, shift2, scale2, gate2 = [mods[l, i] for i in range(6)]
        if l < n_a:
            outs, lses = [], []
            for g, (_, dil) in enumerate(A_GROUPS):
                qkv = _matmul(hs, a_w_qkv, l, BF16, n_off=g * 3 * a_width, n_out=3 * a_width,
                              dil=dil)
                o_g, lse_g = _dilated_attention(qkv, dil, a_heads)
                outs.append(o_g)
                lses.append(lse_g)
            merged = _merge_groups(outs, lses, a_heads)
            mix = _matmul(merged, a_w_o, l, BF16)
        else:
            j = l - n_a
            q = _matmul(hs, b_w_q, j, F32)
            o = _moba_attention(q, k_blocks, vt_blocks, k_mean, slope_lanes)
            mix = _matmul(o, b_w_o, j, BF16)

        wr = jnp.concatenate(
            [moe_w_group[l], jnp.moveaxis(moe_w_expert[l], 0, 1).reshape(d, n_groups * n_per_group),
             jnp.zeros((d, LANES - n_groups * (1 + n_per_group)), F32)], axis=1)
        br = jnp.concatenate(
            [moe_b_group[l], moe_b_expert[l].reshape(-1),
             jnp.zeros((LANES - n_groups * (1 + n_per_group),), F32)]).reshape(1, LANES)
        vec1 = _vec_pack(d, gate=gate1, gain=ln_gain[l, 0], bias=ln_bias[l, 0],
                         scale_a=scale2, shift_a=shift2)
        x2, route, hfp = _ln_router(x2, mix, vec1, wr, br, alpha, n_groups, n_per_group)
        ids = route[:, :MOE_TOPK].astype(jnp.int32)
        y2 = _moe_experts(hfp, ids, l, moe_w_gate, moe_w_up, moe_w_down)

        last = l == depth - 1
        with_kv = l == n_a - 1 and n_a < depth
        rows = dict(gate=gate2, gain=ln_gain[l, 1], bias=ln_bias[l, 1])
        if not last:
            rows.update(scale_a=mods[l + 1, 1], shift_a=mods[l + 1, 0])
        if with_kv:
            rows.update(scale_b=kv_mod[1], shift_b=kv_mod[0])
        res = _ln_combine(x2, y2, route, _vec_pack(d, **rows), alpha, not last, with_kv)
        x2 = res[0]
        if with_kv:
            kv = _matmul(res[1], kv_w[None], 0, F32)
            k_f32 = kv[:, :kv_width]
            k_mean = _block_means(k_f32).reshape(n_blk, kv_heads, HEAD_DIM).transpose(1, 0, 2)
            k_blocks = k_f32.astype(BF16).reshape(n_blk, MOBA_BLOCK, kv_heads, HEAD_DIM) \
                .transpose(2, 0, 1, 3)
            vt_blocks = kv[:, kv_width:].astype(BF16).reshape(n_blk, MOBA_BLOCK, kv_heads, HEAD_DIM) \
                .transpose(2, 0, 3, 1)
            slopes = jnp.exp2(-8.0 * jnp.arange(1, b_heads + 1, dtype=F32) / b_heads)
            slope_lanes = jnp.repeat(slopes.reshape(kv_heads, MOBA_REP), MOBA_BLOCK, axis=1) \
                .reshape(kv_heads, 1, MOBA_REP * MOBA_BLOCK)
        if not last:
            hs = res[-1]
    return x2.reshape(batch, s, d)
```
